```python
import math
import jax, jax.numpy as jnp
from jax import lax
import numpy as np

D_MODEL = 1024
BATCH = 2
SEQ = 8192
DEPTH = 2
DEC_BATCH = 1
DEC_SEQ = 16384
PAST_LEN = 128

N_EVEN = (DEPTH + 1) // 2
N_ODD = DEPTH // 2
EPS = 1e-6

CONV_WIDTH = D_MODEL // 2
CONV_K = 3
MLA_HEADS = 8
QK_NOPE = 64
QK_ROPE = 32
V_DIM = 64
QK_DIM = QK_NOPE + QK_ROPE
Q_LORA = 256
KV_LORA = 128
ROPE_THETA = 10000.0
Q_BLOCK = 128
IN_A_WIDTH = 3 * CONV_WIDTH + Q_LORA + KV_LORA + QK_ROPE
MIX_A_WIDTH = CONV_WIDTH + MLA_HEADS * V_DIM
HYENA_WIDTH = D_MODEL
FILTER_BANDS = 16
FILTER_EMB = 1 + 2 * FILTER_BANDS
FILTER_HIDDEN = 64
DECAY_TARGET = 1e-2
FAST_DECAY_PCT = 0.3
SLOW_DECAY_PCT = 1.5
D_FF = ((8 * D_MODEL + 3 * 256 - 1) // (3 * 256)) * 256

kernel_name = "hybrid_conv_mla_hyena_encoder"


def _rmsnorm(x, g):
    xf = x.astype(jnp.float32)
    y = xf * lax.rsqrt(jnp.mean(xf * xf, axis=-1, keepdims=True) + EPS) * g.astype(jnp.float32)
    return y.astype(x.dtype)


def _dwconv3(x, w):
    xp = jnp.pad(x, ((0, 0), (1, 1), (0, 0)))
    return xp[:, :-2] * w[:, 0] + xp[:, 1:-1] * w[:, 1] + xp[:, 2:] * w[:, 2]


def _rope(x):
    L = x.shape[1]
    half = x.shape[-1] // 2
    inv = ROPE_THETA ** (-jnp.arange(half, dtype=jnp.float32) / half)
    ang = jnp.arange(L, dtype=jnp.float32)[:, None] * inv[None, :]
    cos = jnp.cos(ang)[None, :, None, :]
    sin = jnp.sin(ang)[None, :, None, :]
    x1 = x[..., :half].astype(jnp.float32)
    x2 = x[..., half:].astype(jnp.float32)
    return jnp.concatenate([x1 * cos - x2 * sin, x1 * sin + x2 * cos], axis=-1).astype(x.dtype)


def _block_attention(q, k, v):
    B, L = q.shape[0], q.shape[1]
    nblk = L // Q_BLOCK
    qb = q.reshape(B, nblk, Q_BLOCK, MLA_HEADS, QK_DIM).transpose(1, 0, 2, 3, 4)
    scale = QK_DIM ** -0.5

    def one_block(qi):
        s = jnp.einsum('bqhd,bkhd->bhqk', qi, k).astype(jnp.float32) * scale
        p = jax.nn.softmax(s, axis=-1)
        return jnp.einsum('bhqk,bkhd->bqhd', p.astype(v.dtype), v)

    out = lax.map(one_block, qb)
    return out.transpose(1, 0, 2, 3, 4).reshape(B, L, MLA_HEADS * V_DIM)


def _mixer_conv_mla(h, w_in, conv_w, q_a_norm, kv_a_norm, w_q_up, w_kv_up, q_norm, k_norm, w_out):
    B, L, _ = h.shape
    proj = h @ w_in
    c0 = CONV_WIDTH
    gate_b, gate_c, hv, c_q, c_kv, k_pe = jnp.split(
        proj, [c0, 2 * c0, 3 * c0, 3 * c0 + Q_LORA, 3 * c0 + Q_LORA + KV_LORA], axis=-1)
    y_conv = gate_b * _dwconv3(gate_c * hv, conv_w)
    q = (_rmsnorm(c_q, q_a_norm) @ w_q_up).reshape(B, L, MLA_HEADS, QK_DIM)
    kv = (_rmsnorm(c_kv, kv_a_norm) @ w_kv_up).reshape(B, L, MLA_HEADS, QK_NOPE + V_DIM)
    k_nope, v = kv[..., :QK_NOPE], kv[..., QK_NOPE:]
    k_pe_h = jnp.broadcast_to(k_pe[:, :, None, :], (B, L, MLA_HEADS, QK_ROPE))
    k = jnp.concatenate([k_nope, k_pe_h], axis=-1)
    q = _rmsnorm(q, q_norm)
    k = _rmsnorm(k, k_norm)
    q = jnp.concatenate([q[..., :QK_NOPE], _rope(q[..., QK_NOPE:])], axis=-1)
    k = jnp.concatenate([k[..., :QK_NOPE], _rope(k[..., QK_NOPE:])], axis=-1)
    y_att = _block_attention(q, k, v)
    return jnp.concatenate([y_conv, y_att], axis=-1) @ w_out


def _hyena_filters(L, f_w1, f_b1, f_w2, f_b2, f_w3, f_b3, f_freq, f_w4):
    t = jnp.linspace(0.0, 1.0, L, dtype=jnp.float32)[:, None]
    n = jnp.arange(L, dtype=jnp.float32)[:, None]
    bands = jnp.linspace(1e-4, FILTER_BANDS - 1, FILTER_BANDS, dtype=jnp.float32)[None, :]
    w = 2.0 * math.pi * n / L
    feats = jnp.concatenate([t, jnp.cos(w * bands), -jnp.sin(w * bands)], axis=-1)
    z = jnp.sin(f_freq[0] * (feats @ f_w1 + f_b1))
    z = jnp.sin(f_freq[1] * (z @ f_w2 + f_b2))
    z = jnp.sin(f_freq[2] * (z @ f_w3 + f_b3))
    hfil = (z @ f_w4).astype(jnp.float32).reshape(L, 2, HYENA_WIDTH)
    max_decay = math.log(DECAY_TARGET) / FAST_DECAY_PCT
    min_decay = math.log(DECAY_TARGET) / SLOW_DECAY_PCT
    deltas = jnp.linspace(min_decay, max_decay, HYENA_WIDTH, dtype=jnp.float32)
    window = jnp.exp(-t * jnp.abs(deltas)[None, :])
    hfil = hfil * window[:, None, :]
    return hfil[:, 0], hfil[:, 1]


def _bidir_fftconv(z, h_f, h_b, bias):
    L, C = z.shape[1], z.shape[2]
    kern = jnp.concatenate([h_f, jnp.zeros((1, C), jnp.float32), h_b[1:][::-1]], axis=0)
    k_f = jnp.fft.rfft(kern, n=2 * L, axis=0)
    zf32 = z.astype(jnp.float32)
    z_f = jnp.fft.rfft(zf32, n=2 * L, axis=1)
    y = jnp.fft.irfft(z_f * k_f[None], n=2 * L, axis=1)[:, :L]
    return (y + zf32 * bias.astype(jnp.float32)).astype(z.dtype)


def _mixer_hyena(h, w_in, short_w, f_w1, f_b1, f_w2, f_b2, f_w3, f_b3, f_freq, f_w4, bias, w_out):
    L = h.shape[1]
    u = _dwconv3(h @ w_in, short_w)
    x0, x1, v = jnp.split(u, [HYENA_WIDTH, 2 * HYENA_WIDTH], axis=-1)
    h_f, h_b = _hyena_filters(L, f_w1, f_b1, f_w2, f_b2, f_w3, f_b3, f_freq, f_w4)
    v = _bidir_fftconv(v * x1, h_f, h_b, bias)
    return (v * x0) @ w_out


def _swiglu(h, w_gate, w_up, w_down):
    return (jax.nn.silu(h @ w_gate) * (h @ w_up)) @ w_down


def _trunk(x, mix_norm, ffn_norm, ffn_w_gate, ffn_w_up, ffn_w_down,
           a_w_in, a_conv_w, a_q_a_norm, a_kv_a_norm, a_w_q_up, a_w_kv_up, a_q_norm, a_k_norm, a_w_out,
           c_w_in, c_short_w, c_f_w1, c_f_b1, c_f_w2, c_f_b2, c_f_w3, c_f_b3, c_f_freq, c_f_w4,
           c_bias, c_w_out):
    for layer in range(DEPTH):
        i = layer // 2
        h = _rmsnorm(x, mix_norm[layer])
        if layer % 2 == 0:
            x = x + _mixer_conv_mla(h, a_w_in[i], a_conv_w[i], a_q_a_norm[i], a_kv_a_norm[i],
                                    a_w_q_up[i], a_w_kv_up[i], a_q_norm[i], a_k_norm[i], a_w_out[i])
        else:
            x = x + _mixer_hyena(h, c_w_in[i], c_short_w[i], c_f_w1[i], c_f_b1[i], c_f_w2[i], c_f_b2[i],
                                 c_f_w3[i], c_f_b3[i], c_f_freq[i], c_f_w4[i], c_bias[i], c_w_out[i])
        h = _rmsnorm(x, ffn_norm[layer])
        x = x + _swiglu(h, ffn_w_gate[layer], ffn_w_up[layer], ffn_w_down[layer])
    return x


def setup_inputs(seed: int = 0) -> dict:
    key = jax.random.key(seed)
    ks = jax.random.split(key, 32)
    f32 = jnp.float32

    def w(k, shape, fan_in, scale=1.0):
        return jax.random.normal(k, shape, f32) * (scale * fan_in ** -0.5)

    def gain(k, shape):
        return 1.0 + 0.05 * jax.random.normal(k, shape, f32)

    def small(k, shape, s=0.02):
        return s * jax.random.normal(k, shape, f32)

    D = D_MODEL
    return {
        "x_prompt": jax.random.normal(ks[0], (BATCH, SEQ, D), f32),
        "x_sample": jax.random.normal(ks[1], (DEC_BATCH, DEC_SEQ, D), f32),
        "mix_norm": gain(ks[2], (DEPTH, D)),
        "ffn_norm": gain(ks[3], (DEPTH, D)),
        "ffn_w_gate": w(ks[4], (DEPTH, D, D_FF), D),
        "ffn_w_up": w(ks[5], (DEPTH, D, D_FF), D),
        "ffn_w_down": w(ks[6], (DEPTH, D_FF, D), D_FF),
        "a_w_in": w(ks[7], (N_EVEN, D, IN_A_WIDTH), D),
        "a_conv_w": w(ks[8], (N_EVEN, CONV_WIDTH, CONV_K), CONV_K),
        "a_q_a_norm": gain(ks[9], (N_EVEN, Q_LORA)),
        "a_kv_a_norm": gain(ks[10], (N_EVEN, KV_LORA)),
        "a_w_q_up": w(ks[11], (N_EVEN, Q_LORA, MLA_HEADS * QK_DIM), Q_LORA),
        "a_w_kv_up": w(ks[12], (N_EVEN, KV_LORA, MLA_HEADS * (QK_NOPE + V_DIM)), KV_LORA),
        "a_q_norm": gain(ks[13], (N_EVEN, QK_DIM)),
        "a_k_norm": gain(ks[14], (N_EVEN, QK_DIM)),
        "a_w_out": w(ks[15], (N_EVEN, MIX_A_WIDTH, D), MIX_A_WIDTH),
        "c_w_in": w(ks[16], (N_ODD, D, 3 * HYENA_WIDTH), D),
        "c_short_w": w(ks[17], (N_ODD, 3 * HYENA_WIDTH, CONV_K), CONV_K),
        "c_f_w1": w(ks[18], (N_ODD, FILTER_EMB, FILTER_HIDDEN), FILTER_EMB),
        "c_f_b1": small(ks[19], (N_ODD, FILTER_HIDDEN)),
        "c_f_w2": w(ks[20], (N_ODD, FILTER_HIDDEN, FILTER_HIDDEN), FILTER_HIDDEN),
        "c_f_b2": small(ks[21], (N_ODD, FILTER_HIDDEN)),
        "c_f_w3": w(ks[22], (N_ODD, FILTER_HIDDEN, FILTER_HIDDEN), FILTER_HIDDEN),
        "c_f_b3": small(ks[23], (N_ODD, FILTER_HIDDEN)),
        "c_f_freq": gain(ks[24], (N_ODD, 3, FILTER_HIDDEN)),
        "c_f_w4": w(ks[25], (N_ODD, FILTER_HIDDEN, 2 * HYENA_WIDTH), FILTER_HIDDEN, scale=0.02),
        "c_bias": 1.0 + 0.1 * jax.random.normal(ks[26], (N_ODD, HYENA_WIDTH), f32),
        "c_w_out": w(ks[27], (N_ODD, HYENA_WIDTH, D), HYENA_WIDTH),
    }


def reference(x_prompt, x_sample, mix_norm, ffn_norm, ffn_w_gate, ffn_w_up, ffn_w_down,
              a_w_in, a_conv_w, a_q_a_norm, a_kv_a_norm, a_w_q_up, a_w_kv_up, a_q_norm, a_k_norm, a_w_out,
              c_w_in, c_short_w, c_f_w1, c_f_b1, c_f_w2, c_f_b2, c_f_w3, c_f_b3, c_f_freq, c_f_w4,
              c_bias, c_w_out):
    y_prompt = _trunk(x_prompt, mix_norm, ffn_norm, ffn_w_gate, ffn_w_up, ffn_w_down,
                      a_w_in, a_conv_w, a_q_a_norm, a_kv_a_norm, a_w_q_up, a_w_kv_up, a_q_norm, a_k_norm,
                      a_w_out, c_w_in, c_short_w, c_f_w1, c_f_b1, c_f_w2, c_f_b2, c_f_w3, c_f_b3, c_f_freq,
                      c_f_w4, c_bias, c_w_out)
    y_sample = _trunk(x_sample, mix_norm, ffn_norm, ffn_w_gate, ffn_w_up, ffn_w_down,
                      a_w_in, a_conv_w, a_q_a_norm, a_kv_a_norm, a_w_q_up, a_w_kv_up, a_q_norm, a_k_norm,
                      a_w_out, c_w_in, c_short_w, c_f_w1, c_f_b1, c_f_w2, c_f_b2, c_f_w3, c_f_b3, c_f_freq,
                      c_f_w4, c_bias, c_w_out)
    return (y_prompt, y_sample)
```

```python
import functools
import math

import jax
import jax.numpy as jnp
import numpy as np
from jax import lax
from jax.experimental import pallas as pl
from jax.experimental.pallas import tpu as pltpu

F32 = jnp.float32
BF16 = jnp.bfloat16

D = 1024
EPS = 1e-6
CONV_W = 512
HEADS = 8
QK_NOPE = 64
QK_ROPE = 32
V_DIM = 64
QK_DIM = 96
Q_LORA = 256
KV_LORA = 128
ROPE_THETA = 10000.0
HEAD_PAD = 128
IN_A_PAD = 2048
FILTER_BANDS = 16
FILTER_EMB = 33
FILTER_HIDDEN = 64
D_FF = 2816
FFT_N2 = 256

VMEM_LIMIT_BYTES = 56 * 1024 * 1024


def _params():
    return pltpu.CompilerParams(vmem_limit_bytes=VMEM_LIMIT_BYTES)


def _const_spec(shape):
    zeros = (0,) * len(shape)
    return pl.BlockSpec(shape, lambda *_: zeros, pipeline_mode=pl.Buffered(1))


def _dot(a, b):
    return jnp.dot(a, b, preferred_element_type=F32)


def _rms(x, g):
    ms = jnp.mean(x * x, axis=-1, keepdims=True)
    return x * lax.rsqrt(ms + EPS) * g


def _l0_in_kernel(x_ref, g_ref, w_ref, qan_ref, kvan_ref, wq_ref, wk_ref, wpe_ref, wv_ref,
                  qg_ref, kg_ref, cos_ref, sina_ref, sinb_ref,
                  gb_ref, gg_ref, q_ref, k_ref, v_ref):
    h = _rms(x_ref[...], g_ref[...]).astype(BF16)
    proj = _dot(h, w_ref[...])
    gb_ref[...] = proj[:, 0:CONV_W]
    gg_ref[...] = proj[:, CONV_W:2 * CONV_W] * proj[:, 2 * CONV_W:3 * CONV_W]
    c0 = 3 * CONV_W
    qn = _rms(proj[:, c0:c0 + Q_LORA], qan_ref[...]).astype(BF16)
    kvn = _rms(proj[:, c0 + Q_LORA:c0 + Q_LORA + KV_LORA], kvan_ref[...]).astype(BF16)
    pe = proj[:, c0 + Q_LORA + KV_LORA:IN_A_PAD].astype(BF16)
    qf = _dot(qn, wq_ref[...])
    kf = _dot(kvn, wk_ref[...]) + _dot(pe, wpe_ref[...])
    v_ref[...] = _dot(kvn, wv_ref[...]).astype(BF16)
    cos = cos_ref[...]
    sina = sina_ref[...]
    sinb = sinb_ref[...]
    scale = QK_DIM ** -0.5
    for hd in range(HEADS):
        sl = slice(HEAD_PAD * hd, HEAD_PAD * (hd + 1))
        for full, gain_ref, out_ref, mult in ((qf, qg_ref, q_ref, scale), (kf, kg_ref, k_ref, 1.0)):
            xh = full[:, sl]
            ss = jnp.sum(xh * xh, axis=-1, keepdims=True) * (1.0 / QK_DIM)
            xh = xh * lax.rsqrt(ss + EPS) * gain_ref[...]
            xh = xh * cos + pltpu.roll(xh, HEAD_PAD - 16, 1) * sina + pltpu.roll(xh, 16, 1) * sinb
            out_ref[:, sl] = (xh * mult).astype(BF16)


def _l0_in(x2d, L, g, w_in, qan, kvan, wq, wk, wpe, wv, qg, kg, cos_t, sina_t, sinb_t, tm=512):
    T = x2d.shape[0]
    tps = L // tm
    row = lambda i: (i, 0)
    pos = lambda i: (i % tps, 0)
    return pl.pallas_call(
        _l0_in_kernel,
        grid=(T // tm,),
        in_specs=[
            pl.BlockSpec((tm, D), row),
            _const_spec((1, D)),
            _const_spec((D, IN_A_PAD)),
            _const_spec((1, Q_LORA)),
            _const_spec((1, KV_LORA)),
            _const_spec((Q_LORA, HEADS * HEAD_PAD)),
            _const_spec((KV_LORA, HEADS * HEAD_PAD)),
            _const_spec((HEAD_PAD, HEADS * HEAD_PAD)),
            _const_spec((KV_LORA, HEADS * V_DIM)),
            _const_spec((1, HEAD_PAD)),
            _const_spec((1, HEAD_PAD)),
            pl.BlockSpec((tm, HEAD_PAD), pos),
            pl.BlockSpec((tm, HEAD_PAD), pos),
            pl.BlockSpec((tm, HEAD_PAD), pos),
        ],
        out_specs=[
            pl.BlockSpec((tm, CONV_W), row),
            pl.BlockSpec((tm, CONV_W), row),
            pl.BlockSpec((tm, HEADS * HEAD_PAD), row),
            pl.BlockSpec((tm, HEADS * HEAD_PAD), row),
            pl.BlockSpec((tm, HEADS * V_DIM), row),
        ],
        out_shape=[
            jax.ShapeDtypeStruct((T, CONV_W), F32),
            jax.ShapeDtypeStruct((T, CONV_W), F32),
            jax.ShapeDtypeStruct((T, HEADS * HEAD_PAD), BF16),
            jax.ShapeDtypeStruct((T, HEADS * HEAD_PAD), BF16),
            jax.ShapeDtypeStruct((T, HEADS * V_DIM), BF16),
        ],
        compiler_params=_params(),
        name="l0_in",
    )(x2d, g, w_in, qan, kvan, wq, wk, wpe, wv, qg, kg, cos_t, sina_t, sinb_t)


def _flash_kernel(q_ref, k_ref, v_ref, o_ref, m_ref, l_ref, acc_ref, *, nk):
    kv = pl.program_id(2)
    tq = q_ref.shape[0]

    @pl.when(kv == 0)
    def _():
        m_ref[...] = jnp.full(m_ref.shape, -jnp.inf, F32)
        l_ref[...] = jnp.zeros(l_ref.shape, F32)
        acc_ref[...] = jnp.zeros(acc_ref.shape, F32)

    low_half = lax.broadcasted_iota(jnp.int32, (tq, HEAD_PAD), 1) < V_DIM
    for pair in range(HEADS // 2):
        psl = slice(HEAD_PAD * pair, HEAD_PAD * (pair + 1))
        v_pair = v_ref[:, psl]
        pvs = []
        alphas = []
        for hd in (2 * pair, 2 * pair + 1):
            sl = slice(HEAD_PAD * hd, HEAD_PAD * (hd + 1))
            s = lax.dot_general(q_ref[:, sl], k_ref[:, sl], (((1,), (1,)), ((), ())),
                                preferred_element_type=F32)
            m_prev = m_ref[hd]
            m_new = jnp.maximum(m_prev, jnp.max(s, axis=-1, keepdims=True))
            alpha = jnp.exp(m_prev - m_new)
            p = jnp.exp(s - m_new[:, 0:1])
            l_ref[hd] = alpha * l_ref[hd] + jnp.sum(p, axis=-1, keepdims=True)
            m_ref[hd] = m_new
            pvs.append(_dot(p.astype(BF16), v_pair))
            alphas.append(alpha)
        acc_ref[:, psl] = (acc_ref[:, psl] * jnp.where(low_half, alphas[0], alphas[1])
                           + jnp.where(low_half, pvs[0], pvs[1]))

    @pl.when(kv == nk - 1)
    def _():
        for pair in range(HEADS // 2):
            psl = slice(HEAD_PAD * pair, HEAD_PAD * (pair + 1))
            l_sel = jnp.where(low_half, l_ref[2 * pair], l_ref[2 * pair + 1])
            o_ref[:, psl] = (acc_ref[:, psl] / l_sel).astype(o_ref.dtype)


def _flash(q, k, v, B, L, tq=512, tk=1024):
    tq = min(tq, L)
    tk = min(tk, L)
    nq, nk = L // tq, L // tk
    return pl.pallas_call(
        functools.partial(_flash_kernel, nk=nk),
        grid=(B, nq, nk),
        in_specs=[
            pl.BlockSpec((tq, HEADS * HEAD_PAD), lambda b, i, j: (b * nq + i, 0)),
            pl.BlockSpec((tk, HEADS * HEAD_PAD), lambda b, i, j: (b * nk + j, 0)),
            pl.BlockSpec((tk, HEADS * V_DIM), lambda b, i, j: (b * nk + j, 0)),
        ],
        out_specs=pl.BlockSpec((tq, HEADS * V_DIM), lambda b, i, j: (b * nq + i, 0)),
        out_shape=jax.ShapeDtypeStruct((B * L, HEADS * V_DIM), BF16),
        scratch_shapes=[
            pltpu.VMEM((HEADS, tq, HEAD_PAD), F32),
            pltpu.VMEM((HEADS, tq, HEAD_PAD), F32),
            pltpu.VMEM((tq, HEADS * V_DIM), F32),
        ],
        compiler_params=_params(),
        name="flash_attention",
    )(q, k, v)


def _ffn(x1, fg_ref, wg_ref, wu_ref, wd_ref):
    hn = _rms(x1, fg_ref[...]).astype(BF16)
    g = _dot(hn, wg_ref[...])
    u = _dot(hn, wu_ref[...])
    a = (g * (1.0 / (1.0 + jnp.exp(-g))) * u).astype(BF16)
    return x1 + _dot(a, wd_ref[...])


def _shift_rows(cur, prev_row, next_row):
    tm = cur.shape[0]
    row = lax.broadcasted_iota(jnp.int32, cur.shape, 0)
    before = jnp.where(row == 0, prev_row, pltpu.roll(cur, 1, 0))
    after = jnp.where(row == tm - 1, next_row, pltpu.roll(cur, tm - 1, 0))
    return before, after


def _halo_rows(prev_ref, next_ref, cols, tiles_per_seq):
    i = pl.program_id(0) % tiles_per_seq
    prev_row = jnp.where(i == 0, 0.0, prev_ref[7:8, cols])
    next_row = jnp.where(i == tiles_per_seq - 1, 0.0, next_ref[0:1, cols])
    return prev_row, next_row


def _l0_tail_kernel(x_ref, gb_ref, gg_ref, gprev_ref, gnext_ref, cw_ref, att_ref, wo_ref,
                    fg_ref, wg_ref, wu_ref, wd_ref, o_ref, *, tiles_per_seq):
    gg = gg_ref[...]
    prev_row, next_row = _halo_rows(gprev_ref, gnext_ref, slice(None), tiles_per_seq)
    before, after = _shift_rows(gg, prev_row, next_row)
    cw = cw_ref[...]
    y_conv = gb_ref[...] * (before * cw[0:1] + gg * cw[1:2] + after * cw[2:3])
    mix = _dot(y_conv.astype(BF16), wo_ref[0:CONV_W, :]) + _dot(att_ref[...], wo_ref[CONV_W:, :])
    o_ref[...] = _ffn(x_ref[...] + mix, fg_ref, wg_ref, wu_ref, wd_ref)


def _halo_specs(tm, n_rows, width):
    blocks = tm // 8
    last = n_rows // 8 - 1
    return (pl.BlockSpec((8, width), lambda i: (jnp.maximum(i * blocks - 1, 0), 0)),
            pl.BlockSpec((8, width), lambda i: (jnp.minimum((i + 1) * blocks, last), 0)))


def _ffn_specs():
    return [_const_spec((1, D)), _const_spec((D, D_FF)), _const_spec((D, D_FF)), _const_spec((D_FF, D))]


def _l0_tail(x2d, L, gb, gg, conv_w, att, w_out, fg, wg, wu, wd, tm=256):
    T = x2d.shape[0]
    row = lambda i: (i, 0)
    prev_spec, next_spec = _halo_specs(tm, T, CONV_W)
    return pl.pallas_call(
        functools.partial(_l0_tail_kernel, tiles_per_seq=L // tm),
        grid=(T // tm,),
        in_specs=[
            pl.BlockSpec((tm, D), row),
            pl.BlockSpec((tm, CONV_W), row),
            pl.BlockSpec((tm, CONV_W), row),
            prev_spec,
            next_spec,
            _const_spec((3, CONV_W)),
            pl.BlockSpec((tm, HEADS * V_DIM), row),
            _const_spec((D, D)),
        ] + _ffn_specs(),
        out_specs=pl.BlockSpec((tm, D), row),
        out_shape=jax.ShapeDtypeStruct((T, D), F32),
        compiler_params=_params(),
        name="l0_tail",
    )(x2d, gb, gg, gg, gg, conv_w, att, w_out, fg, wg, wu, wd)


def _l1_tail_kernel(x_ref, y_ref, z_ref, x0_ref, bias_ref, wo_ref, fg_ref, wg_ref, wu_ref, wd_ref, o_ref):
    gated = (y_ref[...] + z_ref[...] * bias_ref[...]) * x0_ref[...]
    x1 = x_ref[...] + _dot(gated.astype(BF16), wo_ref[...])
    o_ref[...] = _ffn(x1, fg_ref, wg_ref, wu_ref, wd_ref)


def _l1_tail(x2d, y, z, x0, bias, w_out, fg, wg, wu, wd, tm=256):
    T = x2d.shape[0]
    row = lambda i: (i, 0)
    tile = pl.BlockSpec((tm, D), row)
    return pl.pallas_call(
        _l1_tail_kernel,
        grid=(T // tm,),
        in_specs=[tile, tile, tile, tile, _const_spec((1, D)), _const_spec((D, D))] + _ffn_specs(),
        out_specs=tile,
        out_shape=jax.ShapeDtypeStruct((T, D), F32),
        compiler_params=_params(),
        name="l1_tail",
    )(x2d, y, z, x0, bias, w_out, fg, wg, wu, wd)


def _norm_matmul_kernel(x_ref, g_ref, w_ref, o_ref):
    h = _rms(x_ref[...], g_ref[...]).astype(BF16)
    o_ref[...] = _dot(h, w_ref[...])


def _norm_matmul(x2d, g, w, tm=512):
    T = x2d.shape[0]
    n = w.shape[1]
    return pl.pallas_call(
        _norm_matmul_kernel,
        grid=(T // tm,),
        in_specs=[pl.BlockSpec((tm, D), lambda i: (i, 0)), _const_spec((1, D)), _const_spec((D, n))],
        out_specs=pl.BlockSpec((tm, n), lambda i: (i, 0)),
        out_shape=jax.ShapeDtypeStruct((T, n), F32),
        compiler_params=_params(),
        name="l1_in",
    )(x2d, g, w)


def _hy_prep_kernel(p_ref, pprev_ref, pnext_ref, sw_ref, x0_ref, z_ref, *, tiles_per_seq):
    def conv(c):
        cols = slice(D * c, D * (c + 1))
        cur = p_ref[:, cols]
        prev_row, next_row = _halo_rows(pprev_ref, pnext_ref, cols, tiles_per_seq)
        before, after = _shift_rows(cur, prev_row, next_row)
        return before * sw_ref[0:1, cols] + cur * sw_ref[1:2, cols] + after * sw_ref[2:3, cols]

    x0_ref[...] = conv(0)
    z_ref[...] = conv(2) * conv(1)


def _hy_prep(p, L, short_w, tm=512):
    T = p.shape[0]
    row = lambda i: (i, 0)
    prev_spec, next_spec = _halo_specs(tm, T, 3 * D)
    return pl.pallas_call(
        functools.partial(_hy_prep_kernel, tiles_per_seq=L // tm),
        grid=(T // tm,),
        in_specs=[pl.BlockSpec((tm, 3 * D), row), prev_spec, next_spec, _const_spec((3, 3 * D))],
        out_specs=[pl.BlockSpec((tm, D), row), pl.BlockSpec((tm, D), row)],
        out_shape=[jax.ShapeDtypeStruct((T, D), F32), jax.ShapeDtypeStruct((T, D), F32)],
        compiler_params=_params(),
        name="hyena_prep",
    )(p, p, p, short_w)


def _filter_kernel(band_ref, phase_ref, w1_ref, b1_ref, w2_ref, b2_ref, w3_ref, b3_ref, fr_ref,
                   w4_ref, decay_ref, o_ref, *, L):
    tr = o_ref.shape[0]
    m = pl.program_id(0) * tr + lax.broadcasted_iota(jnp.int32, (tr, 1), 0)
    j = jnp.where(m < L, m, 2 * L - m).astype(F32)
    t = j / (L - 1.0)
    w = (2.0 * math.pi) * j / L
    lane = lax.broadcasted_iota(jnp.int32, (1, HEAD_PAD), 1)
    feats = jnp.where(lane == 0, t, jnp.cos(w * band_ref[...] + phase_ref[...]))
    hp = lax.Precision.HIGHEST
    z = jnp.sin(fr_ref[0:1, :] * (jnp.dot(feats, w1_ref[...], precision=hp, preferred_element_type=F32)
                                  + b1_ref[...]))
    z = jnp.sin(fr_ref[1:2, :] * (jnp.dot(z, w2_ref[...], precision=hp, preferred_element_type=F32)
                                  + b2_ref[...]))
    z = jnp.sin(fr_ref[2:3, :] * (jnp.dot(z, w3_ref[...], precision=hp, preferred_element_type=F32)
                                  + b3_ref[...]))
    h = jnp.dot(z, w4_ref[...], precision=hp, preferred_element_type=F32)
    h = h * jnp.exp(-t * decay_ref[...])
    sign = jnp.where(m < L, 1.0, jnp.where(m == L, 0.0, -1.0))
    o_ref[...] = h * sign


def _filters(L, band, phase, w1, b1, w2, b2, w3, b3, freq, w4, decay, tr=512):
    nt = 2 * L // tr
    hid = FILTER_HIDDEN
    return pl.pallas_call(
        functools.partial(_filter_kernel, L=L),
        grid=(nt,),
        in_specs=[
            _const_spec((1, HEAD_PAD)), _const_spec((1, HEAD_PAD)),
            _const_spec((HEAD_PAD, hid)), _const_spec((1, hid)),
            _const_spec((hid, hid)), _const_spec((1, hid)),
            _const_spec((hid, hid)), _const_spec((1, hid)),
            _const_spec((3, hid)),
            pl.BlockSpec((hid, D), lambda i: (0, i // (nt // 2))),
            _const_spec((1, D)),
        ],
        out_specs=pl.BlockSpec((tr, D), lambda i: (i, 0)),
        out_shape=jax.ShapeDtypeStruct((2 * L, D), F32),
        compiler_params=_params(),
        name="hyena_filter",
    )(band, phase, w1, b1, w2, b2, w3, b3, freq, w4, decay)


def _stage1_kernel(a_ref, x_ref, o_ref):
    k, r, c = x_ref.shape
    x = x_ref[...].reshape(k * r, c).astype(BF16)
    o_ref[...] = _dot(a_ref[...], x).reshape(o_ref.shape)


def _stage1(a8, x, cols=512):
    B, K, R, _ = x.shape
    M = a8.shape[0] // 8
    return pl.pallas_call(
        _stage1_kernel,
        grid=(B, R // 8, D // cols),
        in_specs=[_const_spec((8 * M, 8 * K)),
                  pl.BlockSpec((None, K, 8, cols), lambda b, j, c: (b, 0, j, c))],
        out_specs=pl.BlockSpec((None, M, 8, cols), lambda b, j, c: (b, 0, j, c)),
        out_shape=jax.ShapeDtypeStruct((B, M, R, D), F32),
        compiler_params=_params(),
        name="fft_stage1",
    )(a8, x)


def _complex_block(m):
    top, bot = m[:FFT_N2], m[FFT_N2:]
    return jnp.concatenate([m, jnp.concatenate([-bot, top], axis=0)], axis=1)


def _filter_spectrum_kernel(mf_ref, x_ref, o_ref):
    x = x_ref[...].reshape(2 * FFT_N2, D).astype(BF16)
    o_ref[...] = _dot(_complex_block(mf_ref[...]), x).reshape(2, FFT_N2, D)


def _filter_spectrum(mf, x1):
    H = x1.shape[2]
    return pl.pallas_call(
        _filter_spectrum_kernel,
        grid=(H,),
        in_specs=[pl.BlockSpec((None, 2 * FFT_N2, FFT_N2), lambda k: (k, 0, 0)),
                  pl.BlockSpec((None, 2, None, FFT_N2, D), lambda k: (0, 0, k, 0, 0))],
        out_specs=pl.BlockSpec((2, None, FFT_N2, D), lambda k: (0, k, 0, 0)),
        out_shape=jax.ShapeDtypeStruct((2, H, FFT_N2, D), F32),
        compiler_params=_params(),
        name="fft_filter_spectrum",
    )(mf, x1)


def _fft_mid_kernel(mf_ref, kf_ref, x_ref, o_ref):
    g = _complex_block(mf_ref[...])
    x = x_ref[...].reshape(2 * FFT_N2, D).astype(BF16)
    zf = _dot(g, x)
    zr, zi = zf[:FFT_N2], zf[FFT_N2:]
    kr, ki = kf_ref[0], kf_ref[1]
    y = jnp.concatenate([zr * kr - zi * ki, zr * ki + zi * kr], axis=0).astype(BF16)
    yi = lax.dot_general(g, y, (((0,), (0,)), ((), ())), preferred_element_type=F32)
    o_ref[...] = yi.reshape(2, FFT_N2, D)


def _fft_mid(mf, kf, x1):
    B, _, H = x1.shape[:3]
    mat = pl.BlockSpec((None, 2 * FFT_N2, FFT_N2), lambda k, b: (k, 0, 0))
    return pl.pallas_call(
        _fft_mid_kernel,
        grid=(H, B),
        in_specs=[mat,
                  pl.BlockSpec((2, None, FFT_N2, D), lambda k, b: (0, k, 0, 0)),
                  pl.BlockSpec((None, 2, None, FFT_N2, D), lambda k, b: (b, 0, k, 0, 0))],
        out_specs=pl.BlockSpec((None, 2, None, FFT_N2, D), lambda k, b: (b, 0, k, 0, 0)),
        out_shape=jax.ShapeDtypeStruct(x1.shape, F32),
        compiler_params=_params(),
        name="fft_mid",
    )(mf, kf, x1)


@functools.lru_cache(maxsize=None)
def _fft_constants(L):
    N = 2 * L
    N1 = N // FFT_N2
    H = N1 // 2
    k1 = np.arange(H, dtype=np.float64) + 0.5
    th = 2.0 * np.pi * np.outer(k1, np.arange(N1, dtype=np.float64)) / N1
    a_full = np.concatenate([np.cos(th), -np.sin(th)], axis=0)
    a_half = a_full[:, :H]
    thi = th[:, :H].T
    b_inv = (2.0 / N) * np.concatenate([np.cos(thi), -np.sin(thi)], axis=1)
    n2 = np.arange(FFT_N2, dtype=np.float64)
    freq = k1[:, None, None] + N1 * n2[None, :, None]
    ang = 2.0 * np.pi * (freq * n2[None, None, :] % N) / N
    mf = np.concatenate([np.cos(ang), -np.sin(ang)], axis=1)
    eye8 = np.eye(8)
    mats = (np.kron(a_half, eye8), np.kron(a_full, eye8), np.kron(b_inv, eye8), mf)
    return tuple(v.astype(np.float32) for v in mats)


def _fft_conv(z, kern, B, L):
    H = L // FFT_N2
    a_half8, a_full8, b_inv8, mf = (jnp.asarray(v).astype(BF16) for v in _fft_constants(L))
    kf = _filter_spectrum(mf, _stage1(a_full8, kern.reshape(1, 2 * H, FFT_N2, D))
                          .reshape(1, 2, H, FFT_N2, D))
    x1 = _stage1(a_half8, z.reshape(B, H, FFT_N2, D)).reshape(B, 2, H, FFT_N2, D)
    y1 = _fft_mid(mf, kf, x1).reshape(B, 2 * H, FFT_N2, D)
    return _stage1(b_inv8, y1).reshape(B * L, D)


def _rope_tables(L):
    half = QK_ROPE // 2
    inv = ROPE_THETA ** (-jnp.arange(half, dtype=F32) / half)
    ang = jnp.arange(L, dtype=F32)[:, None] * inv[None, :]
    cos, sin = jnp.cos(ang), jnp.sin(ang)
    ones = jnp.ones((L, QK_NOPE), F32)
    zeros16 = jnp.zeros((L, half), F32)
    zeros64 = jnp.zeros((L, QK_NOPE), F32)
    pad = jnp.zeros((L, HEAD_PAD - QK_DIM), F32)
    cos_t = jnp.concatenate([ones, cos, cos, pad], axis=1)
    sina_t = jnp.concatenate([zeros64, -sin, zeros16, pad], axis=1)
    sinb_t = jnp.concatenate([zeros64, zeros16, sin, pad], axis=1)
    return cos_t, sina_t, sinb_t


def _pad_heads(w, width):
    k = w.shape[0]
    w = w.reshape(k, HEADS, width)
    return jnp.pad(w, ((0, 0), (0, 0), (0, HEAD_PAD - width))).reshape(k, HEADS * HEAD_PAD)


def _prepare_weights(mix_norm, ffn_norm, ffn_w_gate, ffn_w_up, ffn_w_down,
                     a_w_in, a_conv_w, a_q_a_norm, a_kv_a_norm, a_w_q_up, a_w_kv_up, a_q_norm, a_k_norm,
                     a_w_out, c_w_in, c_short_w, c_f_w1, c_f_b1, c_f_w2, c_f_b2, c_f_w3, c_f_b3, c_f_freq,
                     c_f_w4, c_bias, c_w_out):
    p = {}
    p["mix_g"] = [mix_norm[i].reshape(1, D) for i in range(2)]
    p["ffn"] = [(ffn_norm[i].reshape(1, D), ffn_w_gate[i].astype(BF16), ffn_w_up[i].astype(BF16),
                 ffn_w_down[i].astype(BF16)) for i in range(2)]
    p["a_w_in"] = jnp.pad(a_w_in[0], ((0, 0), (0, IN_A_PAD - a_w_in.shape[2]))).astype(BF16)
    p["a_conv_w"] = a_conv_w[0].T
    p["qan"] = a_q_a_norm[0].reshape(1, Q_LORA)
    p["kvan"] = a_kv_a_norm[0].reshape(1, KV_LORA)
    p["wq"] = _pad_heads(a_w_q_up[0], QK_DIM).astype(BF16)
    kv = a_w_kv_up[0].reshape(KV_LORA, HEADS, QK_NOPE + V_DIM)
    p["wk"] = _pad_heads(kv[:, :, :QK_NOPE].reshape(KV_LORA, HEADS * QK_NOPE), QK_NOPE).astype(BF16)
    p["wv"] = kv[:, :, QK_NOPE:].reshape(KV_LORA, HEADS * V_DIM).astype(BF16)
    place = np.zeros((HEAD_PAD, HEADS, HEAD_PAD), np.float32)
    for r in range(QK_ROPE):
        place[r, :, QK_NOPE + r] = 1.0
    p["wpe"] = jnp.asarray(place.reshape(HEAD_PAD, HEADS * HEAD_PAD)).astype(BF16)
    p["qg"] = jnp.pad(a_q_norm[0], (0, HEAD_PAD - QK_DIM)).reshape(1, HEAD_PAD)
    p["kg"] = jnp.pad(a_k_norm[0], (0, HEAD_PAD - QK_DIM)).reshape(1, HEAD_PAD)
    p["a_w_out"] = a_w_out[0].astype(BF16)
    p["c_w_in"] = c_w_in[0].astype(BF16)
    p["c_short_w"] = c_short_w[0].T
    p["f_w1"] = jnp.pad(c_f_w1[0], ((0, HEAD_PAD - FILTER_EMB), (0, 0)))
    p["f_b"] = [b[0].reshape(1, FILTER_HIDDEN) for b in (c_f_b1, c_f_b2, c_f_b3)]
    p["f_w2"], p["f_w3"], p["f_freq"], p["f_w4"] = c_f_w2[0], c_f_w3[0], c_f_freq[0], c_f_w4[0]
    p["c_bias"] = c_bias[0].reshape(1, D)
    p["c_w_out"] = c_w_out[0].astype(BF16)
    bands = np.linspace(1e-4, FILTER_BANDS - 1, FILTER_BANDS, dtype=np.float32)
    band = np.zeros((1, HEAD_PAD), np.float32)
    phase = np.zeros((1, HEAD_PAD), np.float32)
    band[0, 1:1 + FILTER_BANDS] = bands
    band[0, 1 + FILTER_BANDS:FILTER_EMB] = bands
    phase[0, 1 + FILTER_BANDS:FILTER_EMB] = np.pi / 2
    p["band"], p["phase"] = jnp.asarray(band), jnp.asarray(phase)
    max_decay = math.log(1e-2) / 0.3
    min_decay = math.log(1e-2) / 1.5
    p["decay"] = jnp.asarray(np.abs(np.linspace(min_decay, max_decay, D, dtype=np.float32)).reshape(1, D))
    return p


def _trunk(x, p):
    B, L, _ = x.shape
    x2d = x.reshape(B * L, D)
    cos_t, sina_t, sinb_t = _rope_tables(L)
    gb, gg, q, k, v = _l0_in(x2d, L, p["mix_g"][0], p["a_w_in"], p["qan"], p["kvan"], p["wq"], p["wk"],
                             p["wpe"], p["wv"], p["qg"], p["kg"], cos_t, sina_t, sinb_t)
    att = _flash(q, k, v, B, L)
    x2d = _l0_tail(x2d, L, gb, gg, p["a_conv_w"], att, p["a_w_out"], *p["ffn"][0])
    u = _norm_matmul(x2d, p["mix_g"][1], p["c_w_in"])
    x0, z = _hy_prep(u, L, p["c_short_w"])
    kern = _filters(L, p["band"], p["phase"], p["f_w1"], p["f_b"][0], p["f_w2"], p["f_b"][1], p["f_w3"],
                    p["f_b"][2], p["f_freq"], p["f_w4"], p["decay"])
    y = _fft_conv(z, kern, B, L)
    x2d = _l1_tail(x2d, y, z, x0, p["c_bias"], p["c_w_out"], *p["ffn"][1])
    return x2d.reshape(B, L, D)


def kernel(x_prompt, x_sample, mix_norm, ffn_norm, ffn_w_gate, ffn_w_up, ffn_w_down, a_w_in, a_conv_w, a_q_a_norm, a_kv_a_norm, a_w_q_up, a_w_kv_up, a_q_norm, a_k_norm, a_w_out, c_w_in, c_short_w, c_f_w1, c_f_b1, c_f_w2, c_f_b2, c_f_w3, c_f_b3, c_f_freq, c_f_w4, c_bias, c_w_out):
    p = _prepare_weights(mix_norm, ffn_norm, ffn_w_gate, ffn_w_up, ffn_w_down, a_w_in, a_conv_w,
                         a_q_a_norm, a_kv_a_norm, a_w_q_up, a_w_kv_up, a_q_norm, a_k_norm, a_w_out,
                         c_w_in, c_short_w, c_f_w1, c_f_b1, c_f_w2, c_f_b2, c_f_w3, c_f_b3, c_f_freq,
                         c_f_w4, c_bias, c_w_out)
    return (_trunk(x_prompt, p), _trunk(x_sample, p))
```

```python
import functools
import math

import jax
import jax.numpy as jnp
import numpy as np
from jax import lax
from jax.experimental import pallas as pl
from jax.experimental.pallas import tpu as pltpu

F32 = jnp.float32
BF16 = jnp.bfloat16

D = 1024
EPS = 1e-6
CONV_W = 512
HEADS = 8
QK_NOPE = 64
QK_ROPE = 32
V_DIM = 64
QK_DIM = 96
Q_LORA = 256
KV_LORA = 128
ROPE_THETA = 10000.0
LOG2E = 1.4426950408889634
HEAD_PAD = 128
IN_A_PAD = 2048
FILTER_BANDS = 16
FILTER_EMB = 33
FILTER_HIDDEN = 64
D_FF = 2816
FFN_CHUNKS = 2
FFT_N2 = 256

VMEM_LIMIT_BYTES = 56 * 1024 * 1024


def _params():
    return pltpu.CompilerParams(vmem_limit_bytes=VMEM_LIMIT_BYTES)


def _const_spec(shape):
    zeros = (0,) * len(shape)
    return pl.BlockSpec(shape, lambda *_: zeros, pipeline_mode=pl.Buffered(1))


def _dot(a, b):
    return jnp.dot(a, b, preferred_element_type=F32)


def _rms(x, g):
    ms = jnp.mean(x * x, axis=-1, keepdims=True)
    return x * lax.rsqrt(ms + EPS) * g


def _l0_in_kernel(x_ref, g_ref, w_ref, qan_ref, kvan_ref, wq_ref, wqr_ref, wk_ref, wpe_ref, wper_ref,
                  wv_ref, vone_ref, qg_ref, qgr_ref, kg_ref, kgr_ref, ones_ref, cos_ref, sin_ref,
                  gb_ref, gg_ref, q_ref, k_ref, v_ref):
    h = _rms(x_ref[...], g_ref[...]).astype(BF16)
    proj = _dot(h, w_ref[...])
    gb_ref[...] = proj[:, 0:CONV_W]
    gg_ref[...] = proj[:, CONV_W:2 * CONV_W] * proj[:, 2 * CONV_W:3 * CONV_W]
    c0 = 3 * CONV_W
    qn = _rms(proj[:, c0:c0 + Q_LORA], qan_ref[...]).astype(BF16)
    kvn = _rms(proj[:, c0 + Q_LORA:c0 + Q_LORA + KV_LORA], kvan_ref[...]).astype(BF16)
    pe = proj[:, c0 + Q_LORA + KV_LORA:IN_A_PAD].astype(BF16)
    qf, qr = _dot(qn, wq_ref[...]), _dot(qn, wqr_ref[...])
    kf, kr = _dot(kvn, wk_ref[...]) + _dot(pe, wpe_ref[...]), _dot(pe, wper_ref[...])
    v_ref[...] = (_dot(kvn, wv_ref[...]) + vone_ref[...]).astype(BF16)
    cos, sin = cos_ref[...], sin_ref[...]
    ones = ones_ref[...]
    sides = ((qf, qr, cos * qg_ref[...], sin * qgr_ref[...], q_ref, QK_DIM ** -0.5 * LOG2E),
             (kf, kr, cos * kg_ref[...], sin * kgr_ref[...], k_ref, 1.0))
    for hd in range(HEADS):
        sl = slice(HEAD_PAD * hd, HEAD_PAD * (hd + 1))
        for full, rot, cg, sg, out_ref, mult in sides:
            xh = full[:, sl]
            ss = _dot((xh * xh).astype(BF16), ones) * (1.0 / QK_DIM)
            out_ref[:, sl] = (lax.rsqrt(ss + EPS) * mult * (xh * cg + rot[:, sl] * sg)).astype(BF16)


def _l0_in(x2d, L, p, cos_t, sin_t, tm=512):
    T = x2d.shape[0]
    tps = L // tm
    row = lambda i: (i, 0)
    pos = lambda i: (i % tps, 0)
    wide = HEADS * HEAD_PAD
    return pl.pallas_call(
        _l0_in_kernel,
        grid=(T // tm,),
        in_specs=[
            pl.BlockSpec((tm, D), row),
            _const_spec((1, D)),
            _const_spec((D, IN_A_PAD)),
            _const_spec((1, Q_LORA)),
            _const_spec((1, KV_LORA)),
            _const_spec((Q_LORA, wide)),
            _const_spec((Q_LORA, wide)),
            _const_spec((KV_LORA, wide)),
            _const_spec((HEAD_PAD, wide)),
            _const_spec((HEAD_PAD, wide)),
            _const_spec((KV_LORA, wide)),
            _const_spec((1, wide)),
            _const_spec((1, HEAD_PAD)),
            _const_spec((1, HEAD_PAD)),
            _const_spec((1, HEAD_PAD)),
            _const_spec((1, HEAD_PAD)),
            _const_spec((HEAD_PAD, HEAD_PAD)),
            pl.BlockSpec((tm, HEAD_PAD), pos),
            pl.BlockSpec((tm, HEAD_PAD), pos),
        ],
        out_specs=[
            pl.BlockSpec((tm, CONV_W), row),
            pl.BlockSpec((tm, CONV_W), row),
            pl.BlockSpec((tm, wide), row),
            pl.BlockSpec((tm, wide), row),
            pl.BlockSpec((tm, wide), row),
        ],
        out_shape=[
            jax.ShapeDtypeStruct((T, CONV_W), F32),
            jax.ShapeDtypeStruct((T, CONV_W), F32),
            jax.ShapeDtypeStruct((T, wide), BF16),
            jax.ShapeDtypeStruct((T, wide), BF16),
            jax.ShapeDtypeStruct((T, wide), BF16),
        ],
        compiler_params=_params(),
        name="l0_in",
    )(x2d, p["mix_g"][0], p["a_w_in"], p["qan"], p["kvan"], p["wq"], p["wq_rot"], p["wk"], p["wpe"],
      p["wpe_rot"], p["wv"], p["v_ones"], p["qg"], p["qg_rot"], p["kg"], p["kg_rot"], p["ones"],
      cos_t, sin_t)


def _flash_kernel(q_ref, k_ref, v_ref, o_ref, m_ref, acc_ref, *, nk):
    kv = pl.program_id(2)
    tq = q_ref.shape[0]

    @pl.when(kv == 0)
    def _():
        m_ref[...] = jnp.full(m_ref.shape, -jnp.inf, F32)
        acc_ref[...] = jnp.zeros(acc_ref.shape, F32)

    for hd in range(HEADS):
        sl = slice(HEAD_PAD * hd, HEAD_PAD * (hd + 1))
        s = lax.dot_general(q_ref[:, sl], k_ref[:, sl], (((1,), (1,)), ((), ())),
                            preferred_element_type=F32)
        m_prev = m_ref[hd]
        m_new = jnp.maximum(m_prev, jnp.max(s, axis=-1, keepdims=True))
        p = jnp.exp2(s - m_new[:, 0:1]).astype(BF16)
        acc_ref[:, sl] = acc_ref[:, sl] * jnp.exp2(m_prev - m_new) + _dot(p, v_ref[:, sl])
        m_ref[hd] = m_new

    @pl.when(kv == nk - 1)
    def _():
        low_half = lax.broadcasted_iota(jnp.int32, (tq, HEAD_PAD), 1) < V_DIM
        for pair in range(HEADS // 2):
            a = acc_ref[:, HEAD_PAD * 2 * pair:HEAD_PAD * (2 * pair + 1)]
            b = acc_ref[:, HEAD_PAD * (2 * pair + 1):HEAD_PAD * (2 * pair + 2)]
            o_ref[:, HEAD_PAD * pair:HEAD_PAD * (pair + 1)] = jnp.where(
                low_half, a / pltpu.roll(a, V_DIM, 1), pltpu.roll(b, V_DIM, 1) / b).astype(o_ref.dtype)


def _flash(q, k, v, B, L, tq=1024, tk=1024):
    tq = min(tq, L)
    tk = min(tk, L)
    nq, nk = L // tq, L // tk
    wide = HEADS * HEAD_PAD
    return pl.pallas_call(
        functools.partial(_flash_kernel, nk=nk),
        grid=(B, nq, nk),
        in_specs=[
            pl.BlockSpec((tq, wide), lambda b, i, j: (b * nq + i, 0)),
            pl.BlockSpec((tk, wide), lambda b, i, j: (b * nk + j, 0)),
            pl.BlockSpec((tk, wide), lambda b, i, j: (b * nk + j, 0)),
        ],
        out_specs=pl.BlockSpec((tq, HEADS * V_DIM), lambda b, i, j: (b * nq + i, 0)),
        out_shape=jax.ShapeDtypeStruct((B * L, HEADS * V_DIM), BF16),
        scratch_shapes=[
            pltpu.VMEM((HEADS, tq, HEAD_PAD), F32),
            pltpu.VMEM((tq, wide), F32),
        ],
        compiler_params=_params(),
        name="flash_attention",
    )(q, k, v)


def _ffn(x1, fg_ref, wg_ref, wu_ref, wd_ref):
    hn = _rms(x1, fg_ref[...]).astype(BF16)
    out = x1
    step = D_FF // FFN_CHUNKS
    for c in range(FFN_CHUNKS):
        cols = slice(c * step, (c + 1) * step)
        g = _dot(hn, wg_ref[:, cols])
        u = _dot(hn, wu_ref[:, cols])
        a = (g * (1.0 / (1.0 + jnp.exp(-g))) * u).astype(BF16)
        out = out + _dot(a, wd_ref[cols, :])
    return out


def _shift_rows(cur, prev_row, next_row):
    tm = cur.shape[0]
    row = lax.broadcasted_iota(jnp.int32, cur.shape, 0)
    before = jnp.where(row == 0, prev_row, pltpu.roll(cur, 1, 0))
    after = jnp.where(row == tm - 1, next_row, pltpu.roll(cur, tm - 1, 0))
    return before, after


def _halo_rows(prev_ref, next_ref, cols, tiles_per_seq):
    i = pl.program_id(0) % tiles_per_seq
    prev_row = jnp.where(i == 0, 0.0, prev_ref[7:8, cols])
    next_row = jnp.where(i == tiles_per_seq - 1, 0.0, next_ref[0:1, cols])
    return prev_row, next_row


def _l0_tail_kernel(x_ref, gb_ref, gg_ref, gprev_ref, gnext_ref, cw_ref, att_ref, wo_ref,
                    fg_ref, wg_ref, wu_ref, wd_ref, o_ref, *, tiles_per_seq):
    gg = gg_ref[...]
    prev_row, next_row = _halo_rows(gprev_ref, gnext_ref, slice(None), tiles_per_seq)
    before, after = _shift_rows(gg, prev_row, next_row)
    cw = cw_ref[...]
    y_conv = gb_ref[...] * (before * cw[0:1] + gg * cw[1:2] + after * cw[2:3])
    mix = _dot(y_conv.astype(BF16), wo_ref[0:CONV_W, :]) + _dot(att_ref[...], wo_ref[CONV_W:, :])
    o_ref[...] = _ffn(x_ref[...] + mix, fg_ref, wg_ref, wu_ref, wd_ref)


def _halo_specs(tm, n_rows, width):
    blocks = tm // 8
    last = n_rows // 8 - 1
    return (pl.BlockSpec((8, width), lambda i: (jnp.maximum(i * blocks - 1, 0), 0)),
            pl.BlockSpec((8, width), lambda i: (jnp.minimum((i + 1) * blocks, last), 0)))


def _ffn_specs():
    return [_const_spec((1, D)), _const_spec((D, D_FF)), _const_spec((D, D_FF)), _const_spec((D_FF, D))]


def _l0_tail(x2d, L, gb, gg, conv_w, att, w_out, fg, wg, wu, wd, tm=512):
    T = x2d.shape[0]
    row = lambda i: (i, 0)
    prev_spec, next_spec = _halo_specs(tm, T, CONV_W)
    return pl.pallas_call(
        functools.partial(_l0_tail_kernel, tiles_per_seq=L // tm),
        grid=(T // tm,),
        in_specs=[
            pl.BlockSpec((tm, D), row),
            pl.BlockSpec((tm, CONV_W), row),
            pl.BlockSpec((tm, CONV_W), row),
            prev_spec,
            next_spec,
            _const_spec((3, CONV_W)),
            pl.BlockSpec((tm, HEADS * V_DIM), row),
            _const_spec((D, D)),
        ] + _ffn_specs(),
        out_specs=pl.BlockSpec((tm, D), row),
        out_shape=jax.ShapeDtypeStruct((T, D), F32),
        compiler_params=_params(),
        name="l0_tail",
    )(x2d, gb, gg, gg, gg, conv_w, att, w_out, fg, wg, wu, wd)


def _l1_tail_kernel(x_ref, y_ref, z_ref, x0_ref, bias_ref, wo_ref, fg_ref, wg_ref, wu_ref, wd_ref, o_ref):
    gated = (y_ref[...] + z_ref[...] * bias_ref[...]) * x0_ref[...]
    x1 = x_ref[...] + _dot(gated.astype(BF16), wo_ref[...])
    o_ref[...] = _ffn(x1, fg_ref, wg_ref, wu_ref, wd_ref)


def _l1_tail(x2d, y, z, x0, bias, w_out, fg, wg, wu, wd, tm=512):
    T = x2d.shape[0]
    row = lambda i: (i, 0)
    tile = pl.BlockSpec((tm, D), row)
    return pl.pallas_call(
        _l1_tail_kernel,
        grid=(T // tm,),
        in_specs=[tile, tile, tile, tile, _const_spec((1, D)), _const_spec((D, D))] + _ffn_specs(),
        out_specs=tile,
        out_shape=jax.ShapeDtypeStruct((T, D), F32),
        compiler_params=_params(),
        name="l1_tail",
    )(x2d, y, z, x0, bias, w_out, fg, wg, wu, wd)


def _l1_in_kernel(x_ref, xprev_ref, xnext_ref, g_ref, w_ref, sw_ref, x0_ref, z_ref, *, tiles_per_seq):
    tm = x_ref.shape[0]
    i = pl.program_id(0) % tiles_per_seq
    keep_prev = jnp.where(i == 0, 0.0, 1.0)
    keep_next = jnp.where(i == tiles_per_seq - 1, 0.0, 1.0)
    x_ext = jnp.concatenate([xprev_ref[...] * keep_prev, x_ref[...], xnext_ref[...] * keep_next], axis=0)
    h = _rms(x_ext, g_ref[...]).astype(BF16)
    rows = slice(8, 8 + tm)

    def conv(c):
        cols = slice(D * c, D * (c + 1))
        p = _dot(h, w_ref[:, cols])
        before = pltpu.roll(p, 1, 0)[rows]
        after = pltpu.roll(p, tm + 15, 0)[rows]
        return before * sw_ref[0:1, cols] + p[rows] * sw_ref[1:2, cols] + after * sw_ref[2:3, cols]

    x0_ref[...] = conv(0)
    z_ref[...] = conv(2) * conv(1)


def _l1_in(x2d, L, g, w, short_w, tm=512):
    T = x2d.shape[0]
    row = lambda i: (i, 0)
    prev_spec, next_spec = _halo_specs(tm, T, D)
    return pl.pallas_call(
        functools.partial(_l1_in_kernel, tiles_per_seq=L // tm),
        grid=(T // tm,),
        in_specs=[pl.BlockSpec((tm, D), row), prev_spec, next_spec, _const_spec((1, D)),
                  _const_spec((D, 3 * D)), _const_spec((3, 3 * D))],
        out_specs=[pl.BlockSpec((tm, D), row), pl.BlockSpec((tm, D), row)],
        out_shape=[jax.ShapeDtypeStruct((T, D), F32), jax.ShapeDtypeStruct((T, D), F32)],
        compiler_params=_params(),
        name="l1_in",
    )(x2d, x2d, x2d, g, w, short_w)


def _filter_kernel(band_ref, phase_ref, w1_ref, b1_ref, w2_ref, b2_ref, w3_ref, b3_ref, fr_ref,
                   w4_ref, decay_ref, o_ref, *, L):
    tr = o_ref.shape[0]
    half = tr // 2
    m_lo = pl.program_id(0) * tr + lax.broadcasted_iota(jnp.int32, (half, 1), 0)
    groups = []
    for m in (m_lo, m_lo + half):
        j = jnp.where(m < L, m, 2 * L - m).astype(F32)
        groups.append((m, j / (L - 1.0), (2.0 * math.pi) * j / L))
    lane = lax.broadcasted_iota(jnp.int32, (1, HEAD_PAD), 1)
    first = lane < FILTER_HIDDEN
    t = jnp.where(first, groups[0][1], groups[1][1])
    w = jnp.where(first, groups[0][2], groups[1][2])
    feats = jnp.where((lane & (FILTER_HIDDEN - 1)) == 0, t, jnp.cos(w * band_ref[...] + phase_ref[...]))
    hp = lax.Precision.HIGHEST
    z = jnp.sin(fr_ref[0:1, :] * (jnp.dot(feats, w1_ref[...], precision=hp, preferred_element_type=F32)
                                  + b1_ref[...]))
    z = jnp.sin(fr_ref[1:2, :] * (jnp.dot(z, w2_ref[...], precision=hp, preferred_element_type=F32)
                                  + b2_ref[...]))
    z = jnp.sin(fr_ref[2:3, :] * (jnp.dot(z, w3_ref[...], precision=hp, preferred_element_type=F32)
                                  + b3_ref[...]))
    for g, (m, tg, _) in enumerate(groups):
        zg = jnp.where(first, z, 0.0) if g == 0 else jnp.where(first, 0.0, z)
        h = jnp.dot(zg, w4_ref[...], precision=hp, preferred_element_type=F32)
        sign = jnp.where(m < L, 1.0, jnp.where(m == L, 0.0, -1.0))
        o_ref[g * half:(g + 1) * half, :] = h * jnp.exp(-tg * decay_ref[...]) * sign


def _filters(L, band, phase, w1, b1, w2, b2, w3, b3, freq, w4, decay, tr=512):
    nt = 2 * L // tr
    hid = 2 * FILTER_HIDDEN
    return pl.pallas_call(
        functools.partial(_filter_kernel, L=L),
        grid=(nt,),
        in_specs=[
            _const_spec((1, HEAD_PAD)), _const_spec((1, HEAD_PAD)),
            _const_spec((HEAD_PAD, hid)), _const_spec((1, hid)),
            _const_spec((hid, hid)), _const_spec((1, hid)),
            _const_spec((hid, hid)), _const_spec((1, hid)),
            _const_spec((3, hid)),
            pl.BlockSpec((hid, D), lambda i: (0, i // (nt // 2))),
            _const_spec((1, D)),
        ],
        out_specs=pl.BlockSpec((tr, D), lambda i: (i, 0)),
        out_shape=jax.ShapeDtypeStruct((2 * L, D), F32),
        compiler_params=_params(),
        name="hyena_filter",
    )(band, phase, w1, b1, w2, b2, w3, b3, freq, w4, decay)


def _stage1_kernel(a_ref, x_ref, o_ref):
    k, r, c = x_ref.shape
    x = x_ref[...].reshape(k * r, c).astype(BF16)
    o_ref[...] = _dot(a_ref[...], x).reshape(o_ref.shape)


def _stage1(a8, x, cols=1024):
    B, K, R, _ = x.shape
    M = a8.shape[0] // 8
    return pl.pallas_call(
        _stage1_kernel,
        grid=(B, R // 8, D // cols),
        in_specs=[_const_spec((8 * M, 8 * K)),
                  pl.BlockSpec((None, K, 8, cols), lambda b, j, c: (b, 0, j, c))],
        out_specs=pl.BlockSpec((None, M, 8, cols), lambda b, j, c: (b, 0, j, c)),
        out_shape=jax.ShapeDtypeStruct((B, M, R, D), F32),
        compiler_params=_params(),
        name="fft_stage1",
    )(a8, x)


def _complex_block(m):
    top, bot = m[:FFT_N2], m[FFT_N2:]
    return jnp.concatenate([m, jnp.concatenate([-bot, top], axis=0)], axis=1)


def _filter_spectrum_kernel(mf_ref, x_ref, o_ref):
    x = x_ref[...].reshape(2 * FFT_N2, D).astype(BF16)
    o_ref[...] = _dot(_complex_block(mf_ref[...]), x).reshape(2, FFT_N2, D)


def _filter_spectrum(mf, x1):
    H = x1.shape[2]
    return pl.pallas_call(
        _filter_spectrum_kernel,
        grid=(H,),
        in_specs=[pl.BlockSpec((None, 2 * FFT_N2, FFT_N2), lambda k: (k, 0, 0)),
                  pl.BlockSpec((None, 2, None, FFT_N2, D), lambda k: (0, 0, k, 0, 0))],
        out_specs=pl.BlockSpec((2, None, FFT_N2, D), lambda k: (0, k, 0, 0)),
        out_shape=jax.ShapeDtypeStruct((2, H, FFT_N2, D), F32),
        compiler_params=_params(),
        name="fft_filter_spectrum",
    )(mf, x1)


def _fft_mid_kernel(mf_ref, kf_ref, x_ref, o_ref):
    g = _complex_block(mf_ref[...])
    x = x_ref[...].reshape(2 * FFT_N2, D).astype(BF16)
    zf = _dot(g, x)
    zr, zi = zf[:FFT_N2], zf[FFT_N2:]
    kr, ki = kf_ref[0], kf_ref[1]
    y = jnp.concatenate([zr * kr - zi * ki, zr * ki + zi * kr], axis=0).astype(BF16)
    yi = lax.dot_general(g, y, (((0,), (0,)), ((), ())), preferred_element_type=F32)
    o_ref[...] = yi.reshape(2, FFT_N2, D)


def _fft_mid(mf, kf, x1):
    B, _, H = x1.shape[:3]
    mat = pl.BlockSpec((None, 2 * FFT_N2, FFT_N2), lambda k, b: (k, 0, 0))
    return pl.pallas_call(
        _fft_mid_kernel,
        grid=(H, B),
        in_specs=[mat,
                  pl.BlockSpec((2, None, FFT_N2, D), lambda k, b: (0, k, 0, 0)),
                  pl.BlockSpec((None, 2, None, FFT_N2, D), lambda k, b: (b, 0, k, 0, 0))],
        out_specs=pl.BlockSpec((None, 2, None, FFT_N2, D), lambda k, b: (b, 0, k, 0, 0)),
        out_shape=jax.ShapeDtypeStruct(x1.shape, F32),
        compiler_params=_params(),
        name="fft_mid",
    )(mf, kf, x1)


@functools.lru_cache(maxsize=None)
def _fft_constants(L):
    N = 2 * L
    N1 = N // FFT_N2
    H = N1 // 2
    k1 = np.arange(H, dtype=np.float64) + 0.5
    th = 2.0 * np.pi * np.outer(k1, np.arange(N1, dtype=np.float64)) / N1
    a_full = np.concatenate([np.cos(th), -np.sin(th)], axis=0)
    a_half = a_full[:, :H]
    thi = th[:, :H].T
    b_inv = (2.0 / N) * np.concatenate([np.cos(thi), -np.sin(thi)], axis=1)
    n2 = np.arange(FFT_N2, dtype=np.float64)
    freq = k1[:, None, None] + N1 * n2[None, :, None]
    ang = 2.0 * np.pi * (freq * n2[None, None, :] % N) / N
    mf = np.concatenate([np.cos(ang), -np.sin(ang)], axis=1)
    eye8 = np.eye(8)
    mats = (np.kron(a_half, eye8), np.kron(a_full, eye8), np.kron(b_inv, eye8), mf)
    return tuple(v.astype(np.float32) for v in mats)


def _fft_conv(z, kern, B, L):
    H = L // FFT_N2
    a_half8, a_full8, b_inv8, mf = (jnp.asarray(v).astype(BF16) for v in _fft_constants(L))
    kf = _filter_spectrum(mf, _stage1(a_full8, kern.reshape(1, 2 * H, FFT_N2, D))
                          .reshape(1, 2, H, FFT_N2, D))
    x1 = _stage1(a_half8, z.reshape(B, H, FFT_N2, D)).reshape(B, 2, H, FFT_N2, D)
    y1 = _fft_mid(mf, kf, x1).reshape(B, 2 * H, FFT_N2, D)
    return _stage1(b_inv8, y1).reshape(B * L, D)


def _rope_tables(L):
    half = QK_ROPE // 2
    inv = ROPE_THETA ** (-jnp.arange(half, dtype=F32) / half)
    ang = jnp.arange(L, dtype=F32)[:, None] * inv[None, :]
    cos, sin = jnp.cos(ang), jnp.sin(ang)
    ones = jnp.ones((L, QK_NOPE), F32)
    zeros64 = jnp.zeros((L, QK_NOPE), F32)
    pad = jnp.zeros((L, HEAD_PAD - QK_DIM), F32)
    cos_t = jnp.concatenate([ones, cos, cos, pad], axis=1)
    sin_t = jnp.concatenate([zeros64, sin, sin, pad], axis=1)
    return cos_t, sin_t


def _rotary_partner(w):
    k = w.shape[0]
    half = QK_ROPE // 2
    w = w.reshape(k, HEADS, HEAD_PAD)
    first, second = w[:, :, QK_NOPE:QK_NOPE + half], w[:, :, QK_NOPE + half:QK_DIM]
    out = jnp.concatenate([jnp.zeros((k, HEADS, QK_NOPE), w.dtype), -second, first,
                           jnp.zeros((k, HEADS, HEAD_PAD - QK_DIM), w.dtype)], axis=2)
    return out.reshape(k, HEADS * HEAD_PAD)


def _pad_heads(w, width):
    k = w.shape[0]
    w = w.reshape(k, HEADS, width)
    return jnp.pad(w, ((0, 0), (0, 0), (0, HEAD_PAD - width))).reshape(k, HEADS * HEAD_PAD)


def _prepare_weights(mix_norm, ffn_norm, ffn_w_gate, ffn_w_up, ffn_w_down,
                     a_w_in, a_conv_w, a_q_a_norm, a_kv_a_norm, a_w_q_up, a_w_kv_up, a_q_norm, a_k_norm,
                     a_w_out, c_w_in, c_short_w, c_f_w1, c_f_b1, c_f_w2, c_f_b2, c_f_w3, c_f_b3, c_f_freq,
                     c_f_w4, c_bias, c_w_out):
    p = {}
    p["mix_g"] = [mix_norm[i].reshape(1, D) for i in range(2)]
    p["ffn"] = [(ffn_norm[i].reshape(1, D), ffn_w_gate[i].astype(BF16), ffn_w_up[i].astype(BF16),
                 ffn_w_down[i].astype(BF16)) for i in range(2)]
    p["a_w_in"] = jnp.pad(a_w_in[0], ((0, 0), (0, IN_A_PAD - a_w_in.shape[2]))).astype(BF16)
    p["a_conv_w"] = a_conv_w[0].T
    p["qan"] = a_q_a_norm[0].reshape(1, Q_LORA)
    p["kvan"] = a_kv_a_norm[0].reshape(1, KV_LORA)
    wq = _pad_heads(a_w_q_up[0], QK_DIM)
    p["wq"], p["wq_rot"] = wq.astype(BF16), _rotary_partner(wq).astype(BF16)
    kv = a_w_kv_up[0].reshape(KV_LORA, HEADS, QK_NOPE + V_DIM)
    p["wk"] = _pad_heads(kv[:, :, :QK_NOPE].reshape(KV_LORA, HEADS * QK_NOPE), QK_NOPE).astype(BF16)
    p["wv"] = _pad_heads(kv[:, :, QK_NOPE:].reshape(KV_LORA, HEADS * V_DIM), V_DIM).astype(BF16)
    v_ones = np.zeros((1, HEADS, HEAD_PAD), np.float32)
    v_ones[:, :, V_DIM:] = 1.0
    p["v_ones"] = jnp.asarray(v_ones.reshape(1, HEADS * HEAD_PAD))
    place = np.zeros((HEAD_PAD, HEADS, HEAD_PAD), np.float32)
    for r in range(QK_ROPE):
        place[r, :, QK_NOPE + r] = 1.0
    wpe = jnp.asarray(place.reshape(HEAD_PAD, HEADS * HEAD_PAD))
    p["wpe"], p["wpe_rot"] = wpe.astype(BF16), _rotary_partner(wpe).astype(BF16)
    half = QK_ROPE // 2
    for name, gain in (("qg", a_q_norm[0]), ("kg", a_k_norm[0])):
        p[name] = jnp.pad(gain, (0, HEAD_PAD - QK_DIM)).reshape(1, HEAD_PAD)
        swapped = jnp.concatenate([jnp.zeros((QK_NOPE,), F32), gain[QK_NOPE + half:], gain[QK_NOPE:QK_NOPE + half]])
        p[name + "_rot"] = jnp.pad(swapped, (0, HEAD_PAD - QK_DIM)).reshape(1, HEAD_PAD)
    p["ones"] = jnp.ones((HEAD_PAD, HEAD_PAD), BF16)
    p["a_w_out"] = a_w_out[0].astype(BF16)
    p["c_w_in"] = c_w_in[0].astype(BF16)
    p["c_short_w"] = c_short_w[0].T
    hid = FILTER_HIDDEN
    blockdiag = lambda w: jnp.kron(jnp.eye(2, dtype=F32), w)
    p["f_w1"] = blockdiag(jnp.pad(c_f_w1[0], ((0, hid - FILTER_EMB), (0, 0))))
    p["f_w2"], p["f_w3"] = blockdiag(c_f_w2[0]), blockdiag(c_f_w3[0])
    p["f_b"] = [jnp.tile(b[0].reshape(1, hid), (1, 2)) for b in (c_f_b1, c_f_b2, c_f_b3)]
    p["f_freq"] = jnp.tile(c_f_freq[0], (1, 2))
    p["f_w4"] = jnp.tile(c_f_w4[0], (2, 1))
    p["c_bias"] = c_bias[0].reshape(1, D)
    p["c_w_out"] = c_w_out[0].astype(BF16)
    bands = np.linspace(1e-4, FILTER_BANDS - 1, FILTER_BANDS, dtype=np.float32)
    band = np.zeros((1, hid), np.float32)
    phase = np.zeros((1, hid), np.float32)
    band[0, 1:1 + FILTER_BANDS] = bands
    band[0, 1 + FILTER_BANDS:FILTER_EMB] = bands
    phase[0, 1 + FILTER_BANDS:FILTER_EMB] = np.pi / 2
    p["band"], p["phase"] = jnp.asarray(np.tile(band, (1, 2))), jnp.asarray(np.tile(phase, (1, 2)))
    max_decay = math.log(1e-2) / 0.3
    min_decay = math.log(1e-2) / 1.5
    p["decay"] = jnp.asarray(np.abs(np.linspace(min_decay, max_decay, D, dtype=np.float32)).reshape(1, D))
    return p


def _trunk(x, p):
    B, L, _ = x.shape
    x2d = x.reshape(B * L, D)
    cos_t, sin_t = _rope_tables(L)
    gb, gg, q, k, v = _l0_in(x2d, L, p, cos_t, sin_t)
    att = _flash(q, k, v, B, L)
    x2d = _l0_tail(x2d, L, gb, gg, p["a_conv_w"], att, p["a_w_out"], *p["ffn"][0])
    x0, z = _l1_in(x2d, L, p["mix_g"][1], p["c_w_in"], p["c_short_w"])
    kern = _filters(L, p["band"], p["phase"], p["f_w1"], p["f_b"][0], p["f_w2"], p["f_b"][1], p["f_w3"],
                    p["f_b"][2], p["f_freq"], p["f_w4"], p["decay"])
    y = _fft_conv(z, kern, B, L)
    x2d = _l1_tail(x2d, y, z, x0, p["c_bias"], p["c_w_out"], *p["ffn"][1])
    return x2d.reshape(B, L, D)


def kernel(x_prompt, x_sample, mix_norm, ffn_norm, ffn_w_gate, ffn_w_up, ffn_w_down, a_w_in, a_conv_w, a_q_a_norm, a_kv_a_norm, a_w_q_up, a_w_kv_up, a_q_norm, a_k_norm, a_w_out, c_w_in, c_short_w, c_f_w1, c_f_b1, c_f_w2, c_f_b2, c_f_w3, c_f_b3, c_f_freq, c_f_w4, c_bias, c_w_out):
    p = _prepare_weights(mix_norm, ffn_norm, ffn_w_gate, ffn_w_up, ffn_w_down, a_w_in, a_conv_w,
                         a_q_a_norm, a_kv_a_norm, a_w_q_up, a_w_kv_up, a_q_norm, a_k_norm, a_w_out,
                         c_w_in, c_short_w, c_f_w1, c_f_b1, c_f_w2, c_f_b2, c_f_w3, c_f_b3, c_f_freq,
                         c_f_w4, c_bias, c_w_out)
    return (_trunk(x_prompt, p), _trunk(x_sample, p))
```

```python
import functools
import math

import jax
import jax.numpy as jnp
import numpy as np
from jax import lax
from jax.experimental import pallas as pl
from jax.experimental.pallas import tpu as pltpu

F32 = jnp.float32
BF16 = jnp.bfloat16

D = 1024
EPS = 1e-6
CONV_W = 512
HEADS = 8
QK_NOPE = 64
QK_ROPE = 32
V_DIM = 64
QK_DIM = 96
Q_LORA = 256
KV_LORA = 128
ROPE_THETA = 10000.0
LOG2E = 1.4426950408889634
HEAD_PAD = 128
IN_A_PAD = 2048
FILTER_BANDS = 16
FILTER_EMB = 33
FILTER_HIDDEN = 64
D_FF = 2816
FFN_CHUNKS = 2
FFT_N2 = 256

VMEM_LIMIT_BYTES = 56 * 1024 * 1024


def _params():
    return pltpu.CompilerParams(vmem_limit_bytes=VMEM_LIMIT_BYTES)


def _const_spec(shape):
    zeros = (0,) * len(shape)
    return pl.BlockSpec(shape, lambda *_: zeros, pipeline_mode=pl.Buffered(1))


def _dot(a, b):
    return jnp.dot(a, b, preferred_element_type=F32)


def _rms(x, g):
    ms = jnp.mean(x * x, axis=-1, keepdims=True)
    return x * lax.rsqrt(ms + EPS) * g


def _l0_in_kernel(x_ref, g_ref, w_ref, qan_ref, kvan_ref, wq_ref, wqr_ref, wk_ref, wpe_ref, wper_ref,
                  wv_ref, vone_ref, qg_ref, qgr_ref, kg_ref, kgr_ref, ones_ref, cos_ref, sin_ref,
                  gb_ref, gg_ref, q_ref, k_ref, v_ref):
    h = _rms(x_ref[...], g_ref[...]).astype(BF16)
    proj = _dot(h, w_ref[...])
    gb_ref[...] = proj[:, 0:CONV_W]
    gg_ref[...] = proj[:, CONV_W:2 * CONV_W] * proj[:, 2 * CONV_W:3 * CONV_W]
    c0 = 3 * CONV_W
    qn = _rms(proj[:, c0:c0 + Q_LORA], qan_ref[...]).astype(BF16)
    kvn = _rms(proj[:, c0 + Q_LORA:c0 + Q_LORA + KV_LORA], kvan_ref[...]).astype(BF16)
    pe = proj[:, c0 + Q_LORA + KV_LORA:IN_A_PAD].astype(BF16)
    qf, qr = _dot(qn, wq_ref[...]), _dot(qn, wqr_ref[...])
    kf, kr = _dot(kvn, wk_ref[...]) + _dot(pe, wpe_ref[...]), _dot(pe, wper_ref[...])
    v_ref[...] = (_dot(kvn, wv_ref[...]) + vone_ref[...]).astype(BF16)
    cos, sin = cos_ref[...], sin_ref[...]
    ones = ones_ref[...]
    sides = ((qf, qr, cos * qg_ref[...], sin * qgr_ref[...], q_ref, QK_DIM ** -0.5 * LOG2E),
             (kf, kr, cos * kg_ref[...], sin * kgr_ref[...], k_ref, 1.0))
    for hd in range(HEADS):
        sl = slice(HEAD_PAD * hd, HEAD_PAD * (hd + 1))
        for full, rot, cg, sg, out_ref, mult in sides:
            xh = full[:, sl]
            ss = _dot((xh * xh).astype(BF16), ones) * (1.0 / QK_DIM)
            out_ref[:, sl] = (lax.rsqrt(ss + EPS) * mult * (xh * cg + rot[:, sl] * sg)).astype(BF16)


def _l0_in(x2d, L, p, cos_t, sin_t, tm=512):
    T = x2d.shape[0]
    tps = L // tm
    row = lambda i: (i, 0)
    pos = lambda i: (i % tps, 0)
    wide = HEADS * HEAD_PAD
    return pl.pallas_call(
        _l0_in_kernel,
        grid=(T // tm,),
        in_specs=[
            pl.BlockSpec((tm, D), row),
            _const_spec((1, D)),
            _const_spec((D, IN_A_PAD)),
            _const_spec((1, Q_LORA)),
            _const_spec((1, KV_LORA)),
            _const_spec((Q_LORA, wide)),
            _const_spec((Q_LORA, wide)),
            _const_spec((KV_LORA, wide)),
            _const_spec((HEAD_PAD, wide)),
            _const_spec((HEAD_PAD, wide)),
            _const_spec((KV_LORA, wide)),
            _const_spec((1, wide)),
            _const_spec((1, HEAD_PAD)),
            _const_spec((1, HEAD_PAD)),
            _const_spec((1, HEAD_PAD)),
            _const_spec((1, HEAD_PAD)),
            _const_spec((HEAD_PAD, HEAD_PAD)),
            pl.BlockSpec((tm, HEAD_PAD), pos),
            pl.BlockSpec((tm, HEAD_PAD), pos),
        ],
        out_specs=[
            pl.BlockSpec((tm, CONV_W), row),
            pl.BlockSpec((tm, CONV_W), row),
            pl.BlockSpec((tm, wide), row),
            pl.BlockSpec((tm, wide), row),
            pl.BlockSpec((tm, wide), row),
        ],
        out_shape=[
            jax.ShapeDtypeStruct((T, CONV_W), F32),
            jax.ShapeDtypeStruct((T, CONV_W), F32),
            jax.ShapeDtypeStruct((T, wide), BF16),
            jax.ShapeDtypeStruct((T, wide), BF16),
            jax.ShapeDtypeStruct((T, wide), BF16),
        ],
        compiler_params=_params(),
        name="l0_in",
    )(x2d, p["mix_g"][0], p["a_w_in"], p["qan"], p["kvan"], p["wq"], p["wq_rot"], p["wk"], p["wpe"],
      p["wpe_rot"], p["wv"], p["v_ones"], p["qg"], p["qg_rot"], p["kg"], p["kg_rot"], p["ones"],
      cos_t, sin_t)


def _flash_kernel(q_ref, k_ref, v_ref, o_ref, m_ref, acc_ref, *, nk):
    kv = pl.program_id(2)
    tq = q_ref.shape[0]

    @pl.when(kv == 0)
    def _():
        m_ref[...] = jnp.full(m_ref.shape, -jnp.inf, F32)
        acc_ref[...] = jnp.zeros(acc_ref.shape, F32)

    for hd in range(HEADS):
        sl = slice(HEAD_PAD * hd, HEAD_PAD * (hd + 1))
        s = lax.dot_general(q_ref[:, sl], k_ref[:, sl], (((1,), (1,)), ((), ())),
                            preferred_element_type=F32)
        m_prev = m_ref[hd]
        m_new = jnp.maximum(m_prev, jnp.max(s, axis=-1, keepdims=True))
        p = jnp.exp2((s - m_new[:, 0:1]).astype(BF16))
        acc_ref[:, sl] = acc_ref[:, sl] * jnp.exp2(m_prev - m_new) + _dot(p, v_ref[:, sl])
        m_ref[hd] = m_new

    @pl.when(kv == nk - 1)
    def _():
        low_half = lax.broadcasted_iota(jnp.int32, (tq, HEAD_PAD), 1) < V_DIM
        for pair in range(HEADS // 2):
            a = acc_ref[:, HEAD_PAD * 2 * pair:HEAD_PAD * (2 * pair + 1)]
            b = acc_ref[:, HEAD_PAD * (2 * pair + 1):HEAD_PAD * (2 * pair + 2)]
            o_ref[:, HEAD_PAD * pair:HEAD_PAD * (pair + 1)] = jnp.where(
                low_half, a / pltpu.roll(a, V_DIM, 1), pltpu.roll(b, V_DIM, 1) / b).astype(o_ref.dtype)


def _flash(q, k, v, B, L, tq=2048, tk=512):
    tq = min(tq, L)
    tk = min(tk, L)
    nq, nk = L // tq, L // tk
    wide = HEADS * HEAD_PAD
    return pl.pallas_call(
        functools.partial(_flash_kernel, nk=nk),
        grid=(B, nq, nk),
        in_specs=[
            pl.BlockSpec((tq, wide), lambda b, i, j: (b * nq + i, 0)),
            pl.BlockSpec((tk, wide), lambda b, i, j: (b * nk + j, 0)),
            pl.BlockSpec((tk, wide), lambda b, i, j: (b * nk + j, 0)),
        ],
        out_specs=pl.BlockSpec((tq, HEADS * V_DIM), lambda b, i, j: (b * nq + i, 0)),
        out_shape=jax.ShapeDtypeStruct((B * L, HEADS * V_DIM), BF16),
        scratch_shapes=[
            pltpu.VMEM((HEADS, tq, HEAD_PAD), F32),
            pltpu.VMEM((tq, wide), F32),
        ],
        compiler_params=_params(),
        name="flash_attention",
    )(q, k, v)


def _ffn(x1, fg_ref, wg_ref, wu_ref, wd_ref):
    hn = _rms(x1, fg_ref[...]).astype(BF16)
    out = x1
    step = D_FF // FFN_CHUNKS
    for c in range(FFN_CHUNKS):
        cols = slice(c * step, (c + 1) * step)
        g = _dot(hn, wg_ref[:, cols])
        u = _dot(hn, wu_ref[:, cols])
        a = (g * (1.0 / (1.0 + jnp.exp(-g))) * u).astype(BF16)
        out = out + _dot(a, wd_ref[cols, :])
    return out


def _shift_rows(cur, prev_row, next_row):
    tm = cur.shape[0]
    row = lax.broadcasted_iota(jnp.int32, cur.shape, 0)
    before = jnp.where(row == 0, prev_row, pltpu.roll(cur, 1, 0))
    after = jnp.where(row == tm - 1, next_row, pltpu.roll(cur, tm - 1, 0))
    return before, after


def _halo_rows(prev_ref, next_ref, cols, tiles_per_seq):
    i = pl.program_id(0) % tiles_per_seq
    prev_row = jnp.where(i == 0, 0.0, prev_ref[7:8, cols])
    next_row = jnp.where(i == tiles_per_seq - 1, 0.0, next_ref[0:1, cols])
    return prev_row, next_row


def _l0_tail_kernel(x_ref, gb_ref, gg_ref, gprev_ref, gnext_ref, cw_ref, att_ref, wo_ref,
                    fg_ref, wg_ref, wu_ref, wd_ref, o_ref, *, tiles_per_seq):
    gg = gg_ref[...]
    prev_row, next_row = _halo_rows(gprev_ref, gnext_ref, slice(None), tiles_per_seq)
    before, after = _shift_rows(gg, prev_row, next_row)
    cw = cw_ref[...]
    y_conv = gb_ref[...] * (before * cw[0:1] + gg * cw[1:2] + after * cw[2:3])
    mix = _dot(y_conv.astype(BF16), wo_ref[0:CONV_W, :]) + _dot(att_ref[...], wo_ref[CONV_W:, :])
    o_ref[...] = _ffn(x_ref[...] + mix, fg_ref, wg_ref, wu_ref, wd_ref)


def _halo_specs(tm, n_rows, width):
    blocks = tm // 8
    last = n_rows // 8 - 1
    return (pl.BlockSpec((8, width), lambda i: (jnp.maximum(i * blocks - 1, 0), 0)),
            pl.BlockSpec((8, width), lambda i: (jnp.minimum((i + 1) * blocks, last), 0)))


def _ffn_specs():
    return [_const_spec((1, D)), _const_spec((D, D_FF)), _const_spec((D, D_FF)), _const_spec((D_FF, D))]


def _l0_tail(x2d, L, gb, gg, conv_w, att, w_out, fg, wg, wu, wd, tm=512):
    T = x2d.shape[0]
    row = lambda i: (i, 0)
    prev_spec, next_spec = _halo_specs(tm, T, CONV_W)
    return pl.pallas_call(
        functools.partial(_l0_tail_kernel, tiles_per_seq=L // tm),
        grid=(T // tm,),
        in_specs=[
            pl.BlockSpec((tm, D), row),
            pl.BlockSpec((tm, CONV_W), row),
            pl.BlockSpec((tm, CONV_W), row),
            prev_spec,
            next_spec,
            _const_spec((3, CONV_W)),
            pl.BlockSpec((tm, HEADS * V_DIM), row),
            _const_spec((D, D)),
        ] + _ffn_specs(),
        out_specs=pl.BlockSpec((tm, D), row),
        out_shape=jax.ShapeDtypeStruct((T, D), F32),
        compiler_params=_params(),
        name="l0_tail",
    )(x2d, gb, gg, gg, gg, conv_w, att, w_out, fg, wg, wu, wd)


def _l1_tail_kernel(x_ref, y_ref, z_ref, x0_ref, bias_ref, wo_ref, fg_ref, wg_ref, wu_ref, wd_ref, o_ref):
    gated = (y_ref[...] + z_ref[...] * bias_ref[...]) * x0_ref[...]
    x1 = x_ref[...] + _dot(gated.astype(BF16), wo_ref[...])
    o_ref[...] = _ffn(x1, fg_ref, wg_ref, wu_ref, wd_ref)


def _l1_tail(x2d, y, z, x0, bias, w_out, fg, wg, wu, wd, tm=512):
    T = x2d.shape[0]
    row = lambda i: (i, 0)
    tile = pl.BlockSpec((tm, D), row)
    return pl.pallas_call(
        _l1_tail_kernel,
        grid=(T // tm,),
        in_specs=[tile, tile, tile, tile, _const_spec((1, D)), _const_spec((D, D))] + _ffn_specs(),
        out_specs=tile,
        out_shape=jax.ShapeDtypeStruct((T, D), F32),
        compiler_params=_params(),
        name="l1_tail",
    )(x2d, y, z, x0, bias, w_out, fg, wg, wu, wd)


def _l1_in_kernel(x_ref, xprev_ref, xnext_ref, g_ref, w_ref, sw_ref, x0_ref, z_ref, *, tiles_per_seq):
    tm = x_ref.shape[0]
    i = pl.program_id(0) % tiles_per_seq
    keep_prev = jnp.where(i == 0, 0.0, 1.0)
    keep_next = jnp.where(i == tiles_per_seq - 1, 0.0, 1.0)
    x_ext = jnp.concatenate([xprev_ref[...] * keep_prev, x_ref[...], xnext_ref[...] * keep_next], axis=0)
    h = _rms(x_ext, g_ref[...]).astype(BF16)
    rows = slice(8, 8 + tm)

    def conv(c):
        cols = slice(D * c, D * (c + 1))
        p = _dot(h, w_ref[:, cols])
        before = pltpu.roll(p, 1, 0)[rows]
        after = pltpu.roll(p, tm + 15, 0)[rows]
        return before * sw_ref[0:1, cols] + p[rows] * sw_ref[1:2, cols] + after * sw_ref[2:3, cols]

    x0_ref[...] = conv(0)
    z_ref[...] = conv(2) * conv(1)


def _l1_in(x2d, L, g, w, short_w, tm=512):
    T = x2d.shape[0]
    row = lambda i: (i, 0)
    prev_spec, next_spec = _halo_specs(tm, T, D)
    return pl.pallas_call(
        functools.partial(_l1_in_kernel, tiles_per_seq=L // tm),
        grid=(T // tm,),
        in_specs=[pl.BlockSpec((tm, D), row), prev_spec, next_spec, _const_spec((1, D)),
                  _const_spec((D, 3 * D)), _const_spec((3, 3 * D))],
        out_specs=[pl.BlockSpec((tm, D), row), pl.BlockSpec((tm, D), row)],
        out_shape=[jax.ShapeDtypeStruct((T, D), F32), jax.ShapeDtypeStruct((T, D), F32)],
        compiler_params=_params(),
        name="l1_in",
    )(x2d, x2d, x2d, g, w, short_w)


def _filter_kernel(band_ref, phase_ref, w1_ref, b1_ref, w2_ref, b2_ref, w3_ref, b3_ref, fr_ref,
                   w4_ref, decay_ref, o_ref, *, L):
    tr = o_ref.shape[0]
    half = tr // 2
    m_lo = pl.program_id(0) * tr + lax.broadcasted_iota(jnp.int32, (half, 1), 0)
    groups = []
    for m in (m_lo, m_lo + half):
        j = jnp.where(m < L, m, 2 * L - m).astype(F32)
        groups.append((m, j / (L - 1.0), (2.0 * math.pi) * j / L))
    lane = lax.broadcasted_iota(jnp.int32, (1, HEAD_PAD), 1)
    first = lane < FILTER_HIDDEN
    t = jnp.where(first, groups[0][1], groups[1][1])
    w = jnp.where(first, groups[0][2], groups[1][2])
    feats = jnp.where((lane & (FILTER_HIDDEN - 1)) == 0, t, jnp.cos(w * band_ref[...] + phase_ref[...]))
    hp = lax.Precision.HIGHEST
    z = jnp.sin(fr_ref[0:1, :] * (jnp.dot(feats, w1_ref[...], precision=hp, preferred_element_type=F32)
                                  + b1_ref[...]))
    z = jnp.sin(fr_ref[1:2, :] * (jnp.dot(z, w2_ref[...], precision=hp, preferred_element_type=F32)
                                  + b2_ref[...]))
    z = jnp.sin(fr_ref[2:3, :] * (jnp.dot(z, w3_ref[...], precision=hp, preferred_element_type=F32)
                                  + b3_ref[...]))
    for g, (m, tg, _) in enumerate(groups):
        zg = jnp.where(first, z, 0.0) if g == 0 else jnp.where(first, 0.0, z)
        h = jnp.dot(zg, w4_ref[...], precision=hp, preferred_element_type=F32)
        sign = jnp.where(m < L, 1.0, jnp.where(m == L, 0.0, -1.0))
        o_ref[g * half:(g + 1) * half, :] = h * jnp.exp(-tg * decay_ref[...]) * sign


def _filters(L, band, phase, w1, b1, w2, b2, w3, b3, freq, w4, decay, tr=512):
    nt = 2 * L // tr
    hid = 2 * FILTER_HIDDEN
    return pl.pallas_call(
        functools.partial(_filter_kernel, L=L),
        grid=(nt,),
        in_specs=[
            _const_spec((1, HEAD_PAD)), _const_spec((1, HEAD_PAD)),
            _const_spec((HEAD_PAD, hid)), _const_spec((1, hid)),
            _const_spec((hid, hid)), _const_spec((1, hid)),
            _const_spec((hid, hid)), _const_spec((1, hid)),
            _const_spec((3, hid)),
            pl.BlockSpec((hid, D), lambda i: (0, i // (nt // 2))),
            _const_spec((1, D)),
        ],
        out_specs=pl.BlockSpec((tr, D), lambda i: (i, 0)),
        out_shape=jax.ShapeDtypeStruct((2 * L, D), F32),
        compiler_params=_params(),
        name="hyena_filter",
    )(band, phase, w1, b1, w2, b2, w3, b3, freq, w4, decay)


def _stage1_kernel(a_ref, x_ref, o_ref):
    k, r, c = x_ref.shape
    x = x_ref[...].reshape(k * r, c).astype(BF16)
    o_ref[...] = _dot(a_ref[...], x).reshape(o_ref.shape)


def _stage1(a8, x, cols=1024):
    B, K, R, _ = x.shape
    M = a8.shape[0] // 8
    return pl.pallas_call(
        _stage1_kernel,
        grid=(B, R // 8, D // cols),
        in_specs=[_const_spec((8 * M, 8 * K)),
                  pl.BlockSpec((None, K, 8, cols), lambda b, j, c: (b, 0, j, c))],
        out_specs=pl.BlockSpec((None, M, 8, cols), lambda b, j, c: (b, 0, j, c)),
        out_shape=jax.ShapeDtypeStruct((B, M, R, D), F32),
        compiler_params=_params(),
        name="fft_stage1",
    )(a8, x)


def _complex_block(m):
    top, bot = m[:FFT_N2], m[FFT_N2:]
    return jnp.concatenate([m, jnp.concatenate([-bot, top], axis=0)], axis=1)


def _filter_spectrum_kernel(mf_ref, x_ref, o_ref):
    x = x_ref[...].reshape(2 * FFT_N2, D).astype(BF16)
    o_ref[...] = _dot(_complex_block(mf_ref[...]), x).reshape(2, FFT_N2, D)


def _filter_spectrum(mf, x1):
    H = x1.shape[2]
    return pl.pallas_call(
        _filter_spectrum_kernel,
        grid=(H,),
        in_specs=[pl.BlockSpec((None, 2 * FFT_N2, FFT_N2), lambda k: (k, 0, 0)),
                  pl.BlockSpec((None, 2, None, FFT_N2, D), lambda k: (0, 0, k, 0, 0))],
        out_specs=pl.BlockSpec((2, None, FFT_N2, D), lambda k: (0, k, 0, 0)),
        out_shape=jax.ShapeDtypeStruct((2, H, FFT_N2, D), F32),
        compiler_params=_params(),
        name="fft_filter_spectrum",
    )(mf, x1)


def _fft_mid_kernel(mf_ref, kf_ref, x_ref, o_ref):
    g = _complex_block(mf_ref[...])
    x = x_ref[...].reshape(2 * FFT_N2, D).astype(BF16)
    zf = _dot(g, x)
    zr, zi = zf[:FFT_N2], zf[FFT_N2:]
    kr, ki = kf_ref[0], kf_ref[1]
    y = jnp.concatenate([zr * kr - zi * ki, zr * ki + zi * kr], axis=0).astype(BF16)
    yi = lax.dot_general(g, y, (((0,), (0,)), ((), ())), preferred_element_type=F32)
    o_ref[...] = yi.reshape(2, FFT_N2, D)


def _fft_mid(mf, kf, x1):
    B, _, H = x1.shape[:3]
    mat = pl.BlockSpec((None, 2 * FFT_N2, FFT_N2), lambda k, b: (k, 0, 0))
    return pl.pallas_call(
        _fft_mid_kernel,
        grid=(H, B),
        in_specs=[mat,
                  pl.BlockSpec((2, None, FFT_N2, D), lambda k, b: (0, k, 0, 0)),
                  pl.BlockSpec((None, 2, None, FFT_N2, D), lambda k, b: (b, 0, k, 0, 0))],
        out_specs=pl.BlockSpec((None, 2, None, FFT_N2, D), lambda k, b: (b, 0, k, 0, 0)),
        out_shape=jax.ShapeDtypeStruct(x1.shape, F32),
        compiler_params=_params(),
        name="fft_mid",
    )(mf, kf, x1)


@functools.lru_cache(maxsize=None)
def _fft_constants(L):
    N = 2 * L
    N1 = N // FFT_N2
    H = N1 // 2
    k1 = np.arange(H, dtype=np.float64) + 0.5
    th = 2.0 * np.pi * np.outer(k1, np.arange(N1, dtype=np.float64)) / N1
    a_full = np.concatenate([np.cos(th), -np.sin(th)], axis=0)
    a_half = a_full[:, :H]
    thi = th[:, :H].T
    b_inv = (2.0 / N) * np.concatenate([np.cos(thi), -np.sin(thi)], axis=1)
    n2 = np.arange(FFT_N2, dtype=np.float64)
    freq = k1[:, None, None] + N1 * n2[None, :, None]
    ang = 2.0 * np.pi * (freq * n2[None, None, :] % N) / N
    mf = np.concatenate([np.cos(ang), -np.sin(ang)], axis=1)
    eye8 = np.eye(8)
    mats = (np.kron(a_half, eye8), np.kron(a_full, eye8), np.kron(b_inv, eye8), mf)
    return tuple(v.astype(np.float32) for v in mats)


def _fft_conv(z, kern, B, L):
    H = L // FFT_N2
    a_half8, a_full8, b_inv8, mf = (jnp.asarray(v).astype(BF16) for v in _fft_constants(L))
    kf = _filter_spectrum(mf, _stage1(a_full8, kern.reshape(1, 2 * H, FFT_N2, D))
                          .reshape(1, 2, H, FFT_N2, D))
    x1 = _stage1(a_half8, z.reshape(B, H, FFT_N2, D)).reshape(B, 2, H, FFT_N2, D)
    y1 = _fft_mid(mf, kf, x1).reshape(B, 2 * H, FFT_N2, D)
    return _stage1(b_inv8, y1).reshape(B * L, D)


def _rope_tables(L):
    half = QK_ROPE // 2
    inv = ROPE_THETA ** (-jnp.arange(half, dtype=F32) / half)
    ang = jnp.arange(L, dtype=F32)[:, None] * inv[None, :]
    cos, sin = jnp.cos(ang), jnp.sin(ang)
    ones = jnp.ones((L, QK_NOPE), F32)
    zeros64 = jnp.zeros((L, QK_NOPE), F32)
    pad = jnp.zeros((L, HEAD_PAD - QK_DIM), F32)
    cos_t = jnp.concatenate([ones, cos, cos, pad], axis=1)
    sin_t = jnp.concatenate([zeros64, sin, sin, pad], axis=1)
    return cos_t, sin_t


def _rotary_partner(w):
    k = w.shape[0]
    half = QK_ROPE // 2
    w = w.reshape(k, HEADS, HEAD_PAD)
    first, second = w[:, :, QK_NOPE:QK_NOPE + half], w[:, :, QK_NOPE + half:QK_DIM]
    out = jnp.concatenate([jnp.zeros((k, HEADS, QK_NOPE), w.dtype), -second, first,
                           jnp.zeros((k, HEADS, HEAD_PAD - QK_DIM), w.dtype)], axis=2)
    return out.reshape(k, HEADS * HEAD_PAD)


def _pad_heads(w, width):
    k = w.shape[0]
    w = w.reshape(k, HEADS, width)
    return jnp.pad(w, ((0, 0), (0, 0), (0, HEAD_PAD - width))).reshape(k, HEADS * HEAD_PAD)


def _prepare_weights(mix_norm, ffn_norm, ffn_w_gate, ffn_w_up, ffn_w_down,
                     a_w_in, a_conv_w, a_q_a_norm, a_kv_a_norm, a_w_q_up, a_w_kv_up, a_q_norm, a_k_norm,
                     a_w_out, c_w_in, c_short_w, c_f_w1, c_f_b1, c_f_w2, c_f_b2, c_f_w3, c_f_b3, c_f_freq,
                     c_f_w4, c_bias, c_w_out):
    p = {}
    p["mix_g"] = [mix_norm[i].reshape(1, D) for i in range(2)]
    p["ffn"] = [(ffn_norm[i].reshape(1, D), ffn_w_gate[i].astype(BF16), ffn_w_up[i].astype(BF16),
                 ffn_w_down[i].astype(BF16)) for i in range(2)]
    p["a_w_in"] = jnp.pad(a_w_in[0], ((0, 0), (0, IN_A_PAD - a_w_in.shape[2]))).astype(BF16)
    p["a_conv_w"] = a_conv_w[0].T
    p["qan"] = a_q_a_norm[0].reshape(1, Q_LORA)
    p["kvan"] = a_kv_a_norm[0].reshape(1, KV_LORA)
    wq = _pad_heads(a_w_q_up[0], QK_DIM)
    p["wq"], p["wq_rot"] = wq.astype(BF16), _rotary_partner(wq).astype(BF16)
    kv = a_w_kv_up[0].reshape(KV_LORA, HEADS, QK_NOPE + V_DIM)
    p["wk"] = _pad_heads(kv[:, :, :QK_NOPE].reshape(KV_LORA, HEADS * QK_NOPE), QK_NOPE).astype(BF16)
    p["wv"] = _pad_heads(kv[:, :, QK_NOPE:].reshape(KV_LORA, HEADS * V_DIM), V_DIM).astype(BF16)
    v_ones = np.zeros((1, HEADS, HEAD_PAD), np.float32)
    v_ones[:, :, V_DIM:] = 1.0
    p["v_ones"] = jnp.asarray(v_ones.reshape(1, HEADS * HEAD_PAD))
    place = np.zeros((HEAD_PAD, HEADS, HEAD_PAD), np.float32)
    for r in range(QK_ROPE):
        place[r, :, QK_NOPE + r] = 1.0
    wpe = jnp.asarray(place.reshape(HEAD_PAD, HEADS * HEAD_PAD))
    p["wpe"], p["wpe_rot"] = wpe.astype(BF16), _rotary_partner(wpe).astype(BF16)
    half = QK_ROPE // 2
    for name, gain in (("qg", a_q_norm[0]), ("kg", a_k_norm[0])):
        p[name] = jnp.pad(gain, (0, HEAD_PAD - QK_DIM)).reshape(1, HEAD_PAD)
        swapped = jnp.concatenate([jnp.zeros((QK_NOPE,), F32), gain[QK_NOPE + half:], gain[QK_NOPE:QK_NOPE + half]])
        p[name + "_rot"] = jnp.pad(swapped, (0, HEAD_PAD - QK_DIM)).reshape(1, HEAD_PAD)
    p["ones"] = jnp.ones((HEAD_PAD, HEAD_PAD), BF16)
    p["a_w_out"] = a_w_out[0].astype(BF16)
    p["c_w_in"] = c_w_in[0].astype(BF16)
    p["c_short_w"] = c_short_w[0].T
    hid = FILTER_HIDDEN
    blockdiag = lambda w: jnp.kron(jnp.eye(2, dtype=F32), w)
    p["f_w1"] = blockdiag(jnp.pad(c_f_w1[0], ((0, hid - FILTER_EMB), (0, 0))))
    p["f_w2"], p["f_w3"] = blockdiag(c_f_w2[0]), blockdiag(c_f_w3[0])
    p["f_b"] = [jnp.tile(b[0].reshape(1, hid), (1, 2)) for b in (c_f_b1, c_f_b2, c_f_b3)]
    p["f_freq"] = jnp.tile(c_f_freq[0], (1, 2))
    p["f_w4"] = jnp.tile(c_f_w4[0], (2, 1))
    p["c_bias"] = c_bias[0].reshape(1, D)
    p["c_w_out"] = c_w_out[0].astype(BF16)
    bands = np.linspace(1e-4, FILTER_BANDS - 1, FILTER_BANDS, dtype=np.float32)
    band = np.zeros((1, hid), np.float32)
    phase = np.zeros((1, hid), np.float32)
    band[0, 1:1 + FILTER_BANDS] = bands
    band[0, 1 + FILTER_BANDS:FILTER_EMB] = bands
    phase[0, 1 + FILTER_BANDS:FILTER_EMB] = np.pi / 2
    p["band"], p["phase"] = jnp.asarray(np.tile(band, (1, 2))), jnp.asarray(np.tile(phase, (1, 2)))
    max_decay = math.log(1e-2) / 0.3
    min_decay = math.log(1e-2) / 1.5
    p["decay"] = jnp.asarray(np.abs(np.linspace(min_decay, max_decay, D, dtype=np.float32)).reshape(1, D))
    return p


def _trunk(x, p):
    B, L, _ = x.shape
    x2d = x.reshape(B * L, D)
    cos_t, sin_t = _rope_tables(L)
    gb, gg, q, k, v = _l0_in(x2d, L, p, cos_t, sin_t)
    att = _flash(q, k, v, B, L)
    x2d = _l0_tail(x2d, L, gb, gg, p["a_conv_w"], att, p["a_w_out"], *p["ffn"][0])
    x0, z = _l1_in(x2d, L, p["mix_g"][1], p["c_w_in"], p["c_short_w"])
    kern = _filters(L, p["band"], p["phase"], p["f_w1"], p["f_b"][0], p["f_w2"], p["f_b"][1], p["f_w3"],
                    p["f_b"][2], p["f_freq"], p["f_w4"], p["decay"])
    y = _fft_conv(z, kern, B, L)
    x2d = _l1_tail(x2d, y, z, x0, p["c_bias"], p["c_w_out"], *p["ffn"][1])
    return x2d.reshape(B, L, D)


def kernel(x_prompt, x_sample, mix_norm, ffn_norm, ffn_w_gate, ffn_w_up, ffn_w_down, a_w_in, a_conv_w, a_q_a_norm, a_kv_a_norm, a_w_q_up, a_w_kv_up, a_q_norm, a_k_norm, a_w_out, c_w_in, c_short_w, c_f_w1, c_f_b1, c_f_w2, c_f_b2, c_f_w3, c_f_b3, c_f_freq, c_f_w4, c_bias, c_w_out):
    p = _prepare_weights(mix_norm, ffn_norm, ffn_w_gate, ffn_w_up, ffn_w_down, a_w_in, a_conv_w,
                         a_q_a_norm, a_kv_a_norm, a_w_q_up, a_w_kv_up, a_q_norm, a_k_norm, a_w_out,
                         c_w_in, c_short_w, c_f_w1, c_f_b1, c_f_w2, c_f_b2, c_f_w3, c_f_b3, c_f_freq,
                         c_f_w4, c_bias, c_w_out)
    return (_trunk(x_prompt, p), _trunk(x_sample, p))
```

```python
import functools
import math

import jax
import jax.numpy as jnp
import numpy as np
from jax import lax
from jax.experimental import pallas as pl
from jax.experimental.pallas import tpu as pltpu

F32 = jnp.float32
BF16 = jnp.bfloat16

D = 1024
EPS = 1e-6
CONV_W = 512
HEADS = 8
QK_NOPE = 64
QK_ROPE = 32
V_DIM = 64
QK_DIM = 96
Q_LORA = 256
KV_LORA = 128
ROPE_THETA = 10000.0
LOG2E = 1.4426950408889634
HEAD_PAD = 128
IN_A_PAD = 2048
FILTER_BANDS = 16
FILTER_EMB = 33
FILTER_HIDDEN = 64
D_FF = 2816
FFN_CHUNKS = 2
FFT_N2 = 256

VMEM_LIMIT_BYTES = 56 * 1024 * 1024


def _params():
    return pltpu.CompilerParams(vmem_limit_bytes=VMEM_LIMIT_BYTES)


def _const_spec(shape):
    zeros = (0,) * len(shape)
    return pl.BlockSpec(shape, lambda *_: zeros, pipeline_mode=pl.Buffered(1))


def _dot(a, b):
    return jnp.dot(a, b, preferred_element_type=F32)


def _rms(x, g):
    ms = jnp.mean(x * x, axis=-1, keepdims=True)
    return x * lax.rsqrt(ms + EPS) * g


def _l0_in_kernel(x_ref, g_ref, w_ref, qan_ref, kvan_ref, wq_ref, wqr_ref, wk_ref, wpe_ref, wper_ref,
                  wv_ref, vone_ref, qg_ref, qgr_ref, kg_ref, kgr_ref, ones_ref, cos_ref, sin_ref,
                  gb_ref, gg_ref, q_ref, k_ref, v_ref):
    h = _rms(x_ref[...], g_ref[...]).astype(BF16)
    proj = _dot(h, w_ref[...])
    gb_ref[...] = proj[:, 0:CONV_W]
    gg_ref[...] = proj[:, CONV_W:2 * CONV_W] * proj[:, 2 * CONV_W:3 * CONV_W]
    c0 = 3 * CONV_W
    qn = _rms(proj[:, c0:c0 + Q_LORA], qan_ref[...]).astype(BF16)
    kvn = _rms(proj[:, c0 + Q_LORA:c0 + Q_LORA + KV_LORA], kvan_ref[...]).astype(BF16)
    pe = proj[:, c0 + Q_LORA + KV_LORA:IN_A_PAD].astype(BF16)
    qf, qr = _dot(qn, wq_ref[...]), _dot(qn, wqr_ref[...])
    kf, kr = _dot(kvn, wk_ref[...]) + _dot(pe, wpe_ref[...]), _dot(pe, wper_ref[...])
    v_ref[...] = (_dot(kvn, wv_ref[...]) + vone_ref[...]).astype(BF16)
    cos, sin = cos_ref[...], sin_ref[...]
    ones = ones_ref[...]
    sides = ((qf, qr, cos * qg_ref[...], sin * qgr_ref[...], q_ref, QK_DIM ** -0.5 * LOG2E),
             (kf, kr, cos * kg_ref[...], sin * kgr_ref[...], k_ref, 1.0))
    for hd in range(HEADS):
        sl = slice(HEAD_PAD * hd, HEAD_PAD * (hd + 1))
        for full, rot, cg, sg, out_ref, mult in sides:
            xh = full[:, sl]
            ss = _dot((xh * xh).astype(BF16), ones) * (1.0 / QK_DIM)
            out_ref[:, sl] = (lax.rsqrt(ss + EPS) * mult * (xh * cg + rot[:, sl] * sg)).astype(BF16)


def _l0_in(x2d, L, p, cos_t, sin_t, tm=512):
    T = x2d.shape[0]
    tps = L // tm
    row = lambda i: (i, 0)
    pos = lambda i: (i % tps, 0)
    wide = HEADS * HEAD_PAD
    return pl.pallas_call(
        _l0_in_kernel,
        grid=(T // tm,),
        in_specs=[
            pl.BlockSpec((tm, D), row),
            _const_spec((1, D)),
            _const_spec((D, IN_A_PAD)),
            _const_spec((1, Q_LORA)),
            _const_spec((1, KV_LORA)),
            _const_spec((Q_LORA, wide)),
            _const_spec((Q_LORA, wide)),
            _const_spec((KV_LORA, wide)),
            _const_spec((HEAD_PAD, wide)),
            _const_spec((HEAD_PAD, wide)),
            _const_spec((KV_LORA, wide)),
            _const_spec((1, wide)),
            _const_spec((1, HEAD_PAD)),
            _const_spec((1, HEAD_PAD)),
            _const_spec((1, HEAD_PAD)),
            _const_spec((1, HEAD_PAD)),
            _const_spec((HEAD_PAD, HEAD_PAD)),
            pl.BlockSpec((tm, HEAD_PAD), pos),
            pl.BlockSpec((tm, HEAD_PAD), pos),
        ],
        out_specs=[
            pl.BlockSpec((tm, CONV_W), row),
            pl.BlockSpec((tm, CONV_W), row),
            pl.BlockSpec((tm, wide), row),
            pl.BlockSpec((tm, wide), row),
            pl.BlockSpec((tm, wide), row),
        ],
        out_shape=[
            jax.ShapeDtypeStruct((T, CONV_W), F32),
            jax.ShapeDtypeStruct((T, CONV_W), F32),
            jax.ShapeDtypeStruct((T, wide), BF16),
            jax.ShapeDtypeStruct((T, wide), BF16),
            jax.ShapeDtypeStruct((T, wide), BF16),
        ],
        compiler_params=_params(),
        name="l0_in",
    )(x2d, p["mix_g"][0], p["a_w_in"], p["qan"], p["kvan"], p["wq"], p["wq_rot"], p["wk"], p["wpe"],
      p["wpe_rot"], p["wv"], p["v_ones"], p["qg"], p["qg_rot"], p["kg"], p["kg_rot"], p["ones"],
      cos_t, sin_t)


def _flash_kernel(q_ref, k_ref, v_ref, o_ref, m_ref, acc_ref, *, nk):
    kv = pl.program_id(2)
    tq = q_ref.shape[0]

    @pl.when(kv == 0)
    def _():
        m_ref[...] = jnp.full(m_ref.shape, -jnp.inf, F32)
        acc_ref[...] = jnp.zeros(acc_ref.shape, F32)

    for hd in range(HEADS):
        sl = slice(HEAD_PAD * hd, HEAD_PAD * (hd + 1))
        s = lax.dot_general(q_ref[:, sl], k_ref[:, sl], (((1,), (1,)), ((), ())),
                            preferred_element_type=F32)
        m_prev = m_ref[hd]
        m_new = jnp.maximum(m_prev, jnp.max(s, axis=-1, keepdims=True))
        p = jnp.exp2((s - m_new[:, 0:1]).astype(BF16))
        acc_ref[:, sl] = acc_ref[:, sl] * jnp.exp2(m_prev - m_new) + _dot(p, v_ref[:, sl])
        m_ref[hd] = m_new

    @pl.when(kv == nk - 1)
    def _():
        low_half = lax.broadcasted_iota(jnp.int32, (tq, HEAD_PAD), 1) < V_DIM
        for pair in range(HEADS // 2):
            a = acc_ref[:, HEAD_PAD * 2 * pair:HEAD_PAD * (2 * pair + 1)]
            b = acc_ref[:, HEAD_PAD * (2 * pair + 1):HEAD_PAD * (2 * pair + 2)]
            o_ref[:, HEAD_PAD * pair:HEAD_PAD * (pair + 1)] = jnp.where(
                low_half, a / pltpu.roll(a, V_DIM, 1), pltpu.roll(b, V_DIM, 1) / b).astype(o_ref.dtype)


def _flash(q, k, v, B, L, tq=2048, tk=512):
    tq = min(tq, L)
    tk = min(tk, L)
    nq, nk = L // tq, L // tk
    wide = HEADS * HEAD_PAD
    return pl.pallas_call(
        functools.partial(_flash_kernel, nk=nk),
        grid=(B, nq, nk),
        in_specs=[
            pl.BlockSpec((tq, wide), lambda b, i, j: (b * nq + i, 0)),
            pl.BlockSpec((tk, wide), lambda b, i, j: (b * nk + j, 0)),
            pl.BlockSpec((tk, wide), lambda b, i, j: (b * nk + j, 0)),
        ],
        out_specs=pl.BlockSpec((tq, HEADS * V_DIM), lambda b, i, j: (b * nq + i, 0)),
        out_shape=jax.ShapeDtypeStruct((B * L, HEADS * V_DIM), BF16),
        scratch_shapes=[
            pltpu.VMEM((HEADS, tq, HEAD_PAD), F32),
            pltpu.VMEM((tq, wide), F32),
        ],
        compiler_params=_params(),
        name="flash_attention",
    )(q, k, v)


def _ffn(x1, fg_ref, wg_ref, wu_ref, wd_ref):
    hn = _rms(x1, fg_ref[...]).astype(BF16)
    out = x1
    step = D_FF // FFN_CHUNKS
    for c in range(FFN_CHUNKS):
        cols = slice(c * step, (c + 1) * step)
        g = _dot(hn, wg_ref[:, cols])
        u = _dot(hn, wu_ref[:, cols])
        a = (g * (1.0 / (1.0 + jnp.exp(-g))) * u).astype(BF16)
        out = out + _dot(a, wd_ref[cols, :])
    return out


def _shift_rows(cur, prev_row, next_row):
    tm = cur.shape[0]
    row = lax.broadcasted_iota(jnp.int32, cur.shape, 0)
    before = jnp.where(row == 0, prev_row, pltpu.roll(cur, 1, 0))
    after = jnp.where(row == tm - 1, next_row, pltpu.roll(cur, tm - 1, 0))
    return before, after


def _halo_rows(prev_ref, next_ref, cols, tiles_per_seq):
    i = pl.program_id(0) % tiles_per_seq
    prev_row = jnp.where(i == 0, 0.0, prev_ref[7:8, cols])
    next_row = jnp.where(i == tiles_per_seq - 1, 0.0, next_ref[0:1, cols])
    return prev_row, next_row


def _l0_tail_kernel(x_ref, gb_ref, gg_ref, gprev_ref, gnext_ref, cw_ref, att_ref, wo_ref,
                    fg_ref, wg_ref, wu_ref, wd_ref, o_ref, *, tiles_per_seq):
    gg = gg_ref[...]
    prev_row, next_row = _halo_rows(gprev_ref, gnext_ref, slice(None), tiles_per_seq)
    before, after = _shift_rows(gg, prev_row, next_row)
    cw = cw_ref[...]
    y_conv = gb_ref[...] * (before * cw[0:1] + gg * cw[1:2] + after * cw[2:3])
    mix = _dot(y_conv.astype(BF16), wo_ref[0:CONV_W, :]) + _dot(att_ref[...], wo_ref[CONV_W:, :])
    o_ref[...] = _ffn(x_ref[...] + mix, fg_ref, wg_ref, wu_ref, wd_ref)


def _halo_specs(tm, n_rows, width):
    blocks = tm // 8
    last = n_rows // 8 - 1
    return (pl.BlockSpec((8, width), lambda i: (jnp.maximum(i * blocks - 1, 0), 0)),
            pl.BlockSpec((8, width), lambda i: (jnp.minimum((i + 1) * blocks, last), 0)))


def _ffn_specs():
    return [_const_spec((1, D)), _const_spec((D, D_FF)), _const_spec((D, D_FF)), _const_spec((D_FF, D))]


def _l0_tail(x2d, L, gb, gg, conv_w, att, w_out, fg, wg, wu, wd, tm=512):
    T = x2d.shape[0]
    row = lambda i: (i, 0)
    prev_spec, next_spec = _halo_specs(tm, T, CONV_W)
    return pl.pallas_call(
        functools.partial(_l0_tail_kernel, tiles_per_seq=L // tm),
        grid=(T // tm,),
        in_specs=[
            pl.BlockSpec((tm, D), row),
            pl.BlockSpec((tm, CONV_W), row),
            pl.BlockSpec((tm, CONV_W), row),
            prev_spec,
            next_spec,
            _const_spec((3, CONV_W)),
            pl.BlockSpec((tm, HEADS * V_DIM), row),
            _const_spec((D, D)),
        ] + _ffn_specs(),
        out_specs=pl.BlockSpec((tm, D), row),
        out_shape=jax.ShapeDtypeStruct((T, D), F32),
        compiler_params=_params(),
        name="l0_tail",
    )(x2d, gb, gg, gg, gg, conv_w, att, w_out, fg, wg, wu, wd)


def _l1_tail_kernel(x_ref, y_ref, z_ref, x0_ref, bias_ref, wo_ref, fg_ref, wg_ref, wu_ref, wd_ref, o_ref):
    gated = (y_ref[...] + z_ref[...] * bias_ref[...]) * x0_ref[...]
    x1 = x_ref[...] + _dot(gated.astype(BF16), wo_ref[...])
    o_ref[...] = _ffn(x1, fg_ref, wg_ref, wu_ref, wd_ref)


def _l1_tail(x2d, y, z, x0, bias, w_out, fg, wg, wu, wd, tm=512):
    T = x2d.shape[0]
    row = lambda i: (i, 0)
    tile = pl.BlockSpec((tm, D), row)
    return pl.pallas_call(
        _l1_tail_kernel,
        grid=(T // tm,),
        in_specs=[tile, tile, tile, tile, _const_spec((1, D)), _const_spec((D, D))] + _ffn_specs(),
        out_specs=tile,
        out_shape=jax.ShapeDtypeStruct((T, D), F32),
        compiler_params=_params(),
        name="l1_tail",
    )(x2d, y, z, x0, bias, w_out, fg, wg, wu, wd)


def _l1_in_kernel(x_ref, xprev_ref, xnext_ref, g_ref, w_ref, sw_ref, x0_ref, z_ref, *, tiles_per_seq):
    tm = x_ref.shape[0]
    i = pl.program_id(0) % tiles_per_seq
    keep_prev = jnp.where(i == 0, 0.0, 1.0)
    keep_next = jnp.where(i == tiles_per_seq - 1, 0.0, 1.0)
    x_ext = jnp.concatenate([xprev_ref[...] * keep_prev, x_ref[...], xnext_ref[...] * keep_next], axis=0)
    h = _rms(x_ext, g_ref[...]).astype(BF16)
    rows = slice(8, 8 + tm)

    def conv(c):
        cols = slice(D * c, D * (c + 1))
        p = _dot(h, w_ref[:, cols])
        before = pltpu.roll(p, 1, 0)[rows]
        after = pltpu.roll(p, tm + 15, 0)[rows]
        return before * sw_ref[0:1, cols] + p[rows] * sw_ref[1:2, cols] + after * sw_ref[2:3, cols]

    x0_ref[...] = conv(0)
    z_ref[...] = conv(2) * conv(1)


def _l1_in(x2d, L, g, w, short_w, tm=512):
    T = x2d.shape[0]
    row = lambda i: (i, 0)
    prev_spec, next_spec = _halo_specs(tm, T, D)
    return pl.pallas_call(
        functools.partial(_l1_in_kernel, tiles_per_seq=L // tm),
        grid=(T // tm,),
        in_specs=[pl.BlockSpec((tm, D), row), prev_spec, next_spec, _const_spec((1, D)),
                  _const_spec((D, 3 * D)), _const_spec((3, 3 * D))],
        out_specs=[pl.BlockSpec((tm, D), row), pl.BlockSpec((tm, D), row)],
        out_shape=[jax.ShapeDtypeStruct((T, D), F32), jax.ShapeDtypeStruct((T, D), F32)],
        compiler_params=_params(),
        name="l1_in",
    )(x2d, x2d, x2d, g, w, short_w)


def _filter_kernel(band_ref, phase_ref, w1_ref, b1_ref, w2_ref, b2_ref, w3_ref, b3_ref, fr_ref,
                   w4h_ref, w4l_ref, decay_ref, o_ref, *, L):
    tr = o_ref.shape[1]
    half = tr // 2
    m_lo = pl.program_id(0) * tr + lax.broadcasted_iota(jnp.int32, (half, 1), 0)
    groups = []
    for m in (m_lo, m_lo + half):
        j = m.astype(F32)
        groups.append((m, j / (L - 1.0), (2.0 * math.pi) * j / L))
    lane = lax.broadcasted_iota(jnp.int32, (1, HEAD_PAD), 1)
    first = lane < FILTER_HIDDEN
    t = jnp.where(first, groups[0][1], groups[1][1])
    w = jnp.where(first, groups[0][2], groups[1][2])
    feats = jnp.where((lane & (FILTER_HIDDEN - 1)) == 0, t, jnp.cos(w * band_ref[...] + phase_ref[...]))
    hp = lax.Precision.HIGHEST
    z = jnp.sin(fr_ref[0:1, :] * (jnp.dot(feats, w1_ref[...], precision=hp, preferred_element_type=F32)
                                  + b1_ref[...]))
    z = jnp.sin(fr_ref[1:2, :] * (jnp.dot(z, w2_ref[...], precision=hp, preferred_element_type=F32)
                                  + b2_ref[...]))
    z = jnp.sin(fr_ref[2:3, :] * (jnp.dot(z, w3_ref[...], precision=hp, preferred_element_type=F32)
                                  + b3_ref[...]))
    for g, (m, tg, _) in enumerate(groups):
        zg = jnp.where(first, z, 0.0) if g == 0 else jnp.where(first, 0.0, z)
        z_hi = zg.astype(BF16)
        z_lo = (zg - z_hi.astype(F32)).astype(BF16)
        h = _dot(z_hi, w4h_ref[...]) + _dot(z_hi, w4l_ref[...]) + _dot(z_lo, w4h_ref[...])
        window = jnp.exp(-tg * decay_ref[...])
        rows = slice(g * half, (g + 1) * half)
        o_ref[0, rows, :] = h[:, :D] * window
        o_ref[1, rows, :] = h[:, D:] * window * jnp.where(m == 0, 0.0, 1.0)


def _filters(L, band, phase, w1, b1, w2, b2, w3, b3, freq, w4_hi, w4_lo, decay, tr=512):
    hid = 2 * FILTER_HIDDEN
    return pl.pallas_call(
        functools.partial(_filter_kernel, L=L),
        grid=(L // tr,),
        in_specs=[
            _const_spec((1, HEAD_PAD)), _const_spec((1, HEAD_PAD)),
            _const_spec((HEAD_PAD, hid)), _const_spec((1, hid)),
            _const_spec((hid, hid)), _const_spec((1, hid)),
            _const_spec((hid, hid)), _const_spec((1, hid)),
            _const_spec((3, hid)),
            _const_spec((hid, 2 * D)),
            _const_spec((hid, 2 * D)),
            _const_spec((1, D)),
        ],
        out_specs=pl.BlockSpec((2, tr, D), lambda i: (0, i, 0)),
        out_shape=jax.ShapeDtypeStruct((2, L, D), F32),
        compiler_params=_params(),
        name="hyena_filter",
    )(band, phase, w1, b1, w2, b2, w3, b3, freq, w4_hi, w4_lo, decay)


def _stage1_kernel(a_ref, x_ref, o_ref):
    k, r, c = x_ref.shape
    x = x_ref[...].reshape(k * r, c).astype(BF16)
    o_ref[...] = _dot(a_ref[...], x).reshape(o_ref.shape)


def _stage1(a8, x, cols=1024):
    B, K, R, _ = x.shape
    M = a8.shape[0] // 8
    return pl.pallas_call(
        _stage1_kernel,
        grid=(B, R // 8, D // cols),
        in_specs=[_const_spec((8 * M, 8 * K)),
                  pl.BlockSpec((None, K, 8, cols), lambda b, j, c: (b, 0, j, c))],
        out_specs=pl.BlockSpec((None, M, 8, cols), lambda b, j, c: (b, 0, j, c)),
        out_shape=jax.ShapeDtypeStruct((B, M, R, D), F32),
        compiler_params=_params(),
        name="fft_stage1",
    )(a8, x)


def _complex_block(m):
    top, bot = m[:FFT_N2], m[FFT_N2:]
    return jnp.concatenate([m, jnp.concatenate([-bot, top], axis=0)], axis=1)


def _filter_spectrum_kernel(mf_ref, x_ref, o_ref):
    g = _complex_block(mf_ref[...])
    zf = _dot(g, x_ref[0].reshape(2 * FFT_N2, D).astype(BF16))
    zb = _dot(g, x_ref[1].reshape(2 * FFT_N2, D).astype(BF16))
    o_ref[0] = zf[:FFT_N2] + zb[:FFT_N2]
    o_ref[1] = zf[FFT_N2:] - zb[FFT_N2:]


def _filter_spectrum(mf, x1):
    H = x1.shape[2]
    return pl.pallas_call(
        _filter_spectrum_kernel,
        grid=(H,),
        in_specs=[pl.BlockSpec((None, 2 * FFT_N2, FFT_N2), lambda k: (k, 0, 0)),
                  pl.BlockSpec((2, 2, None, FFT_N2, D), lambda k: (0, 0, k, 0, 0))],
        out_specs=pl.BlockSpec((2, None, FFT_N2, D), lambda k: (0, k, 0, 0)),
        out_shape=jax.ShapeDtypeStruct((2, H, FFT_N2, D), F32),
        compiler_params=_params(),
        name="fft_filter_spectrum",
    )(mf, x1)


def _fft_mid_kernel(mf_ref, kf_ref, x_ref, o_ref):
    g = _complex_block(mf_ref[...])
    x = x_ref[...].reshape(2 * FFT_N2, D).astype(BF16)
    zf = _dot(g, x)
    zr, zi = zf[:FFT_N2], zf[FFT_N2:]
    kr, ki = kf_ref[0], kf_ref[1]
    y = jnp.concatenate([zr * kr - zi * ki, zr * ki + zi * kr], axis=0).astype(BF16)
    yi = lax.dot_general(g, y, (((0,), (0,)), ((), ())), preferred_element_type=F32)
    o_ref[...] = yi.reshape(2, FFT_N2, D)


def _fft_mid(mf, kf, x1):
    B, _, H = x1.shape[:3]
    mat = pl.BlockSpec((None, 2 * FFT_N2, FFT_N2), lambda k, b: (k, 0, 0))
    return pl.pallas_call(
        _fft_mid_kernel,
        grid=(H, B),
        in_specs=[mat,
                  pl.BlockSpec((2, None, FFT_N2, D), lambda k, b: (0, k, 0, 0)),
                  pl.BlockSpec((None, 2, None, FFT_N2, D), lambda k, b: (b, 0, k, 0, 0))],
        out_specs=pl.BlockSpec((None, 2, None, FFT_N2, D), lambda k, b: (b, 0, k, 0, 0)),
        out_shape=jax.ShapeDtypeStruct(x1.shape, F32),
        compiler_params=_params(),
        name="fft_mid",
    )(mf, kf, x1)


@functools.lru_cache(maxsize=None)
def _fft_constants(L):
    N = 2 * L
    N1 = N // FFT_N2
    H = N1 // 2
    k1 = np.arange(H, dtype=np.float64) + 0.5
    th = 2.0 * np.pi * np.outer(k1, np.arange(N1, dtype=np.float64)) / N1
    a_full = np.concatenate([np.cos(th), -np.sin(th)], axis=0)
    a_half = a_full[:, :H]
    thi = th[:, :H].T
    b_inv = (2.0 / N) * np.concatenate([np.cos(thi), -np.sin(thi)], axis=1)
    n2 = np.arange(FFT_N2, dtype=np.float64)
    freq = k1[:, None, None] + N1 * n2[None, :, None]
    ang = 2.0 * np.pi * (freq * n2[None, None, :] % N) / N
    mf = np.concatenate([np.cos(ang), -np.sin(ang)], axis=1)
    eye8 = np.eye(8)
    mats = (np.kron(a_half, eye8), np.kron(b_inv, eye8), mf)
    return tuple(v.astype(np.float32) for v in mats)


def _fft_conv(z, filt, B, L):
    H = L // FFT_N2
    a_half8, b_inv8, mf = (jnp.asarray(v).astype(BF16) for v in _fft_constants(L))
    kf = _filter_spectrum(mf, _stage1(a_half8, filt.reshape(2, H, FFT_N2, D)).reshape(2, 2, H, FFT_N2, D))
    x1 = _stage1(a_half8, z.reshape(B, H, FFT_N2, D)).reshape(B, 2, H, FFT_N2, D)
    y1 = _fft_mid(mf, kf, x1).reshape(B, 2 * H, FFT_N2, D)
    return _stage1(b_inv8, y1).reshape(B * L, D)


def _rope_tables(L):
    half = QK_ROPE // 2
    inv = ROPE_THETA ** (-jnp.arange(half, dtype=F32) / half)
    ang = jnp.arange(L, dtype=F32)[:, None] * inv[None, :]
    cos, sin = jnp.cos(ang), jnp.sin(ang)
    ones = jnp.ones((L, QK_NOPE), F32)
    zeros64 = jnp.zeros((L, QK_NOPE), F32)
    pad = jnp.zeros((L, HEAD_PAD - QK_DIM), F32)
    cos_t = jnp.concatenate([ones, cos, cos, pad], axis=1)
    sin_t = jnp.concatenate([zeros64, sin, sin, pad], axis=1)
    return cos_t, sin_t


def _rotary_partner(w):
    k = w.shape[0]
    half = QK_ROPE // 2
    w = w.reshape(k, HEADS, HEAD_PAD)
    first, second = w[:, :, QK_NOPE:QK_NOPE + half], w[:, :, QK_NOPE + half:QK_DIM]
    out = jnp.concatenate([jnp.zeros((k, HEADS, QK_NOPE), w.dtype), -second, first,
                           jnp.zeros((k, HEADS, HEAD_PAD - QK_DIM), w.dtype)], axis=2)
    return out.reshape(k, HEADS * HEAD_PAD)


def _pad_heads(w, width):
    k = w.shape[0]
    w = w.reshape(k, HEADS, width)
    return jnp.pad(w, ((0, 0), (0, 0), (0, HEAD_PAD - width))).reshape(k, HEADS * HEAD_PAD)


def _prepare_weights(mix_norm, ffn_norm, ffn_w_gate, ffn_w_up, ffn_w_down,
                     a_w_in, a_conv_w, a_q_a_norm, a_kv_a_norm, a_w_q_up, a_w_kv_up, a_q_norm, a_k_norm,
                     a_w_out, c_w_in, c_short_w, c_f_w1, c_f_b1, c_f_w2, c_f_b2, c_f_w3, c_f_b3, c_f_freq,
                     c_f_w4, c_bias, c_w_out):
    p = {}
    p["mix_g"] = [mix_norm[i].reshape(1, D) for i in range(2)]
    p["ffn"] = [(ffn_norm[i].reshape(1, D), ffn_w_gate[i].astype(BF16), ffn_w_up[i].astype(BF16),
                 ffn_w_down[i].astype(BF16)) for i in range(2)]
    p["a_w_in"] = jnp.pad(a_w_in[0], ((0, 0), (0, IN_A_PAD - a_w_in.shape[2]))).astype(BF16)
    p["a_conv_w"] = a_conv_w[0].T
    p["qan"] = a_q_a_norm[0].reshape(1, Q_LORA)
    p["kvan"] = a_kv_a_norm[0].reshape(1, KV_LORA)
    wq = _pad_heads(a_w_q_up[0], QK_DIM)
    p["wq"], p["wq_rot"] = wq.astype(BF16), _rotary_partner(wq).astype(BF16)
    kv = a_w_kv_up[0].reshape(KV_LORA, HEADS, QK_NOPE + V_DIM)
    p["wk"] = _pad_heads(kv[:, :, :QK_NOPE].reshape(KV_LORA, HEADS * QK_NOPE), QK_NOPE).astype(BF16)
    p["wv"] = _pad_heads(kv[:, :, QK_NOPE:].reshape(KV_LORA, HEADS * V_DIM), V_DIM).astype(BF16)
    v_ones = np.zeros((1, HEADS, HEAD_PAD), np.float32)
    v_ones[:, :, V_DIM:] = 1.0
    p["v_ones"] = jnp.asarray(v_ones.reshape(1, HEADS * HEAD_PAD))
    place = np.zeros((HEAD_PAD, HEADS, HEAD_PAD), np.float32)
    for r in range(QK_ROPE):
        place[r, :, QK_NOPE + r] = 1.0
    wpe = jnp.asarray(place.reshape(HEAD_PAD, HEADS * HEAD_PAD))
    p["wpe"], p["wpe_rot"] = wpe.astype(BF16), _rotary_partner(wpe).astype(BF16)
    half = QK_ROPE // 2
    for name, gain in (("qg", a_q_norm[0]), ("kg", a_k_norm[0])):
        p[name] = jnp.pad(gain, (0, HEAD_PAD - QK_DIM)).reshape(1, HEAD_PAD)
        swapped = jnp.concatenate([jnp.zeros((QK_NOPE,), F32), gain[QK_NOPE + half:], gain[QK_NOPE:QK_NOPE + half]])
        p[name + "_rot"] = jnp.pad(swapped, (0, HEAD_PAD - QK_DIM)).reshape(1, HEAD_PAD)
    p["ones"] = jnp.ones((HEAD_PAD, HEAD_PAD), BF16)
    p["a_w_out"] = a_w_out[0].astype(BF16)
    p["c_w_in"] = c_w_in[0].astype(BF16)
    p["c_short_w"] = c_short_w[0].T
    hid = FILTER_HIDDEN
    blockdiag = lambda w: jnp.kron(jnp.eye(2, dtype=F32), w)
    p["f_w1"] = blockdiag(jnp.pad(c_f_w1[0], ((0, hid - FILTER_EMB), (0, 0))))
    p["f_w2"], p["f_w3"] = blockdiag(c_f_w2[0]), blockdiag(c_f_w3[0])
    p["f_b"] = [jnp.tile(b[0].reshape(1, hid), (1, 2)) for b in (c_f_b1, c_f_b2, c_f_b3)]
    p["f_freq"] = jnp.tile(c_f_freq[0], (1, 2))
    w4 = jnp.tile(c_f_w4[0], (2, 1))
    p["f_w4_hi"] = w4.astype(BF16)
    p["f_w4_lo"] = (w4 - p["f_w4_hi"].astype(F32)).astype(BF16)
    p["c_bias"] = c_bias[0].reshape(1, D)
    p["c_w_out"] = c_w_out[0].astype(BF16)
    bands = np.linspace(1e-4, FILTER_BANDS - 1, FILTER_BANDS, dtype=np.float32)
    band = np.zeros((1, hid), np.float32)
    phase = np.zeros((1, hid), np.float32)
    band[0, 1:1 + FILTER_BANDS] = bands
    band[0, 1 + FILTER_BANDS:FILTER_EMB] = bands
    phase[0, 1 + FILTER_BANDS:FILTER_EMB] = np.pi / 2
    p["band"], p["phase"] = jnp.asarray(np.tile(band, (1, 2))), jnp.asarray(np.tile(phase, (1, 2)))
    max_decay = math.log(1e-2) / 0.3
    min_decay = math.log(1e-2) / 1.5
    p["decay"] = jnp.asarray(np.abs(np.linspace(min_decay, max_decay, D, dtype=np.float32)).reshape(1, D))
    return p


def _trunk(x, p):
    B, L, _ = x.shape
    x2d = x.reshape(B * L, D)
    cos_t, sin_t = _rope_tables(L)
    gb, gg, q, k, v = _l0_in(x2d, L, p, cos_t, sin_t)
    att = _flash(q, k, v, B, L)
    x2d = _l0_tail(x2d, L, gb, gg, p["a_conv_w"], att, p["a_w_out"], *p["ffn"][0])
    x0, z = _l1_in(x2d, L, p["mix_g"][1], p["c_w_in"], p["c_short_w"])
    filt = _filters(L, p["band"], p["phase"], p["f_w1"], p["f_b"][0], p["f_w2"], p["f_b"][1], p["f_w3"],
                    p["f_b"][2], p["f_freq"], p["f_w4_hi"], p["f_w4_lo"], p["decay"])
    y = _fft_conv(z, filt, B, L)
    x2d = _l1_tail(x2d, y, z, x0, p["c_bias"], p["c_w_out"], *p["ffn"][1])
    return x2d.reshape(B, L, D)


def kernel(x_prompt, x_sample, mix_norm, ffn_norm, ffn_w_gate, ffn_w_up, ffn_w_down, a_w_in, a_conv_w, a_q_a_norm, a_kv_a_norm, a_w_q_up, a_w_kv_up, a_q_norm, a_k_norm, a_w_out, c_w_in, c_short_w, c_f_w1, c_f_b1, c_f_w2, c_f_b2, c_f_w3, c_f_b3, c_f_freq, c_f_w4, c_bias, c_w_out):
    p = _prepare_weights(mix_norm, ffn_norm, ffn_w_gate, ffn_w_up, ffn_w_down, a_w_in, a_conv_w,
                         a_q_a_norm, a_kv_a_norm, a_w_q_up, a_w_kv_up, a_q_norm, a_k_norm, a_w_out,
                         c_w_in, c_short_w, c_f_w1, c_f_b1, c_f_w2, c_f_b2, c_f_w3, c_f_b3, c_f_freq,
                         c_f_w4, c_bias, c_w_out)
    return (_trunk(x_prompt, p), _trunk(x_sample, p))
```

```python
import functools
import math

import jax
import jax.numpy as jnp
import numpy as np
from jax import lax
from jax.experimental import pallas as pl
from jax.experimental.pallas import tpu as pltpu

F32 = jnp.float32
BF16 = jnp.bfloat16

D = 1024
EPS = 1e-6
CONV_W = 512
HEADS = 8
QK_NOPE = 64
QK_ROPE = 32
V_DIM = 64
QK_DIM = 96
Q_LORA = 256
KV_LORA = 128
ROPE_THETA = 10000.0
LOG2E = 1.4426950408889634
HEAD_PAD = 128
IN_A_PAD = 2048
FILTER_BANDS = 16
FILTER_EMB = 33
FILTER_HIDDEN = 64
D_FF = 2816
MXU_DIM = 256
FFN_SPLITS = (0, (D_FF // MXU_DIM + 1) // 2 * MXU_DIM, D_FF)
FFT_N2 = 256

VMEM_LIMIT_BYTES = 56 * 1024 * 1024


def _params():
    return pltpu.CompilerParams(vmem_limit_bytes=VMEM_LIMIT_BYTES)


def _const_spec(shape):
    zeros = (0,) * len(shape)
    return pl.BlockSpec(shape, lambda *_: zeros, pipeline_mode=pl.Buffered(1))


def _dot(a, b):
    return jnp.dot(a, b, preferred_element_type=F32)


def _rms(x, g):
    ms = jnp.mean(x * x, axis=-1, keepdims=True)
    return x * lax.rsqrt(ms + EPS) * g


def _l0_in_kernel(x_ref, g_ref, w_ref, qan_ref, kvan_ref, wq_ref, wqr_ref, wk_ref, wpe_ref, wper_ref,
                  wv_ref, vone_ref, qg_ref, qgr_ref, kg_ref, kgr_ref, ones_ref, cos_ref, sin_ref,
                  gb_ref, gg_ref, q_ref, k_ref, v_ref):
    h = _rms(x_ref[...], g_ref[...]).astype(BF16)
    proj = _dot(h, w_ref[...])
    gb_ref[...] = proj[:, 0:CONV_W]
    gg_ref[...] = proj[:, CONV_W:2 * CONV_W] * proj[:, 2 * CONV_W:3 * CONV_W]
    c0 = 3 * CONV_W
    qn = _rms(proj[:, c0:c0 + Q_LORA], qan_ref[...]).astype(BF16)
    kvn = _rms(proj[:, c0 + Q_LORA:c0 + Q_LORA + KV_LORA], kvan_ref[...]).astype(BF16)
    pe = proj[:, c0 + Q_LORA + KV_LORA:IN_A_PAD].astype(BF16)
    qf, qr = _dot(qn, wq_ref[...]), _dot(qn, wqr_ref[...])
    kf, kr = _dot(kvn, wk_ref[...]) + _dot(pe, wpe_ref[...]), _dot(pe, wper_ref[...])
    v_ref[...] = (_dot(kvn, wv_ref[...]) + vone_ref[...]).astype(BF16)
    cos, sin = cos_ref[...], sin_ref[...]
    ones = ones_ref[...]
    sides = ((qf, qr, cos * qg_ref[...], sin * qgr_ref[...], q_ref, QK_DIM ** -0.5 * LOG2E),
             (kf, kr, cos * kg_ref[...], sin * kgr_ref[...], k_ref, 1.0))
    for hd in range(HEADS):
        sl = slice(HEAD_PAD * hd, HEAD_PAD * (hd + 1))
        for full, rot, cg, sg, out_ref, mult in sides:
            xh = full[:, sl]
            ss = _dot((xh * xh).astype(BF16), ones) * (1.0 / QK_DIM)
            out_ref[:, sl] = (lax.rsqrt(ss + EPS) * mult * (xh * cg + rot[:, sl] * sg)).astype(BF16)


def _l0_in(x2d, L, p, cos_t, sin_t, tm=512):
    T = x2d.shape[0]
    tps = L // tm
    row = lambda i: (i, 0)
    pos = lambda i: (i % tps, 0)
    wide = HEADS * HEAD_PAD
    return pl.pallas_call(
        _l0_in_kernel,
        grid=(T // tm,),
        in_specs=[
            pl.BlockSpec((tm, D), row),
            _const_spec((1, D)),
            _const_spec((D, IN_A_PAD)),
            _const_spec((1, Q_LORA)),
            _const_spec((1, KV_LORA)),
            _const_spec((Q_LORA, wide)),
            _const_spec((Q_LORA, wide)),
            _const_spec((KV_LORA, wide)),
            _const_spec((HEAD_PAD, wide)),
            _const_spec((HEAD_PAD, wide)),
            _const_spec((KV_LORA, wide)),
            _const_spec((1, wide)),
            _const_spec((1, HEAD_PAD)),
            _const_spec((1, HEAD_PAD)),
            _const_spec((1, HEAD_PAD)),
            _const_spec((1, HEAD_PAD)),
            _const_spec((HEAD_PAD, HEAD_PAD)),
            pl.BlockSpec((tm, HEAD_PAD), pos),
            pl.BlockSpec((tm, HEAD_PAD), pos),
        ],
        out_specs=[
            pl.BlockSpec((tm, CONV_W), row),
            pl.BlockSpec((tm, CONV_W), row),
            pl.BlockSpec((tm, wide), row),
            pl.BlockSpec((tm, wide), row),
            pl.BlockSpec((tm, wide), row),
        ],
        out_shape=[
            jax.ShapeDtypeStruct((T, CONV_W), F32),
            jax.ShapeDtypeStruct((T, CONV_W), F32),
            jax.ShapeDtypeStruct((T, wide), BF16),
            jax.ShapeDtypeStruct((T, wide), BF16),
            jax.ShapeDtypeStruct((T, wide), BF16),
        ],
        compiler_params=_params(),
        name="l0_in",
    )(x2d, p["mix_g"][0], p["a_w_in"], p["qan"], p["kvan"], p["wq"], p["wq_rot"], p["wk"], p["wpe"],
      p["wpe_rot"], p["wv"], p["v_ones"], p["qg"], p["qg_rot"], p["kg"], p["kg_rot"], p["ones"],
      cos_t, sin_t)


def _flash_kernel(q_ref, k_ref, v_ref, o_ref, m_ref, acc_ref, *, nk):
    kv = pl.program_id(2)
    tq = q_ref.shape[0]

    @pl.when(kv == 0)
    def _():
        m_ref[...] = jnp.full(m_ref.shape, -jnp.inf, F32)
        acc_ref[...] = jnp.zeros(acc_ref.shape, F32)

    for hd in range(HEADS):
        sl = slice(HEAD_PAD * hd, HEAD_PAD * (hd + 1))
        s = lax.dot_general(q_ref[:, sl], k_ref[:, sl], (((1,), (1,)), ((), ())),
                            preferred_element_type=F32)
        m_prev = m_ref[hd]
        m_new = jnp.maximum(m_prev, jnp.max(s, axis=-1, keepdims=True))
        p = jnp.exp2((s - m_new[:, 0:1]).astype(BF16))
        acc_ref[:, sl] = acc_ref[:, sl] * jnp.exp2(m_prev - m_new) + _dot(p, v_ref[:, sl])
        m_ref[hd] = m_new

    @pl.when(kv == nk - 1)
    def _():
        low_half = lax.broadcasted_iota(jnp.int32, (tq, HEAD_PAD), 1) < V_DIM
        for pair in range(HEADS // 2):
            a = acc_ref[:, HEAD_PAD * 2 * pair:HEAD_PAD * (2 * pair + 1)]
            b = acc_ref[:, HEAD_PAD * (2 * pair + 1):HEAD_PAD * (2 * pair + 2)]
            o_ref[:, HEAD_PAD * pair:HEAD_PAD * (pair + 1)] = jnp.where(
                low_half, a / pltpu.roll(a, V_DIM, 1), pltpu.roll(b, V_DIM, 1) / b).astype(o_ref.dtype)


def _flash(q, k, v, B, L, tq=2048, tk=512):
    tq = min(tq, L)
    tk = min(tk, L)
    nq, nk = L // tq, L // tk
    wide = HEADS * HEAD_PAD
    return pl.pallas_call(
        functools.partial(_flash_kernel, nk=nk),
        grid=(B, nq, nk),
        in_specs=[
            pl.BlockSpec((tq, wide), lambda b, i, j: (b * nq + i, 0)),
            pl.BlockSpec((tk, wide), lambda b, i, j: (b * nk + j, 0)),
            pl.BlockSpec((tk, wide), lambda b, i, j: (b * nk + j, 0)),
        ],
        out_specs=pl.BlockSpec((tq, HEADS * V_DIM), lambda b, i, j: (b * nq + i, 0)),
        out_shape=jax.ShapeDtypeStruct((B * L, HEADS * V_DIM), BF16),
        scratch_shapes=[
            pltpu.VMEM((HEADS, tq, HEAD_PAD), F32),
            pltpu.VMEM((tq, wide), F32),
        ],
        compiler_params=_params(),
        name="flash_attention",
    )(q, k, v)


def _ffn(x1, fg_ref, wg_ref, wu_ref, wd_ref):
    hn = _rms(x1, fg_ref[...]).astype(BF16)
    out = x1
    for lo, hi in zip(FFN_SPLITS[:-1], FFN_SPLITS[1:]):
        cols = slice(lo, hi)
        g = _dot(hn, wg_ref[:, cols])
        u = _dot(hn, wu_ref[:, cols])
        a = (g * (1.0 / (1.0 + jnp.exp(-g))) * u).astype(BF16)
        out = out + _dot(a, wd_ref[cols, :])
    return out


def _shift_rows(cur, prev_row, next_row):
    tm = cur.shape[0]
    row = lax.broadcasted_iota(jnp.int32, cur.shape, 0)
    before = jnp.where(row == 0, prev_row, pltpu.roll(cur, 1, 0))
    after = jnp.where(row == tm - 1, next_row, pltpu.roll(cur, tm - 1, 0))
    return before, after


def _halo_rows(prev_ref, next_ref, cols, tiles_per_seq):
    i = pl.program_id(0) % tiles_per_seq
    prev_row = jnp.where(i == 0, 0.0, prev_ref[7:8, cols])
    next_row = jnp.where(i == tiles_per_seq - 1, 0.0, next_ref[0:1, cols])
    return prev_row, next_row


def _l0_tail_kernel(x_ref, gb_ref, gg_ref, gprev_ref, gnext_ref, cw_ref, att_ref, wo_ref,
                    fg_ref, wg_ref, wu_ref, wd_ref, o_ref, *, tiles_per_seq):
    gg = gg_ref[...]
    prev_row, next_row = _halo_rows(gprev_ref, gnext_ref, slice(None), tiles_per_seq)
    before, after = _shift_rows(gg, prev_row, next_row)
    cw = cw_ref[...]
    y_conv = gb_ref[...] * (before * cw[0:1] + gg * cw[1:2] + after * cw[2:3])
    mix = _dot(y_conv.astype(BF16), wo_ref[0:CONV_W, :]) + _dot(att_ref[...], wo_ref[CONV_W:, :])
    o_ref[...] = _ffn(x_ref[...] + mix, fg_ref, wg_ref, wu_ref, wd_ref)


def _halo_specs(tm, n_rows, width):
    blocks = tm // 8
    last = n_rows // 8 - 1
    return (pl.BlockSpec((8, width), lambda i: (jnp.maximum(i * blocks - 1, 0), 0)),
            pl.BlockSpec((8, width), lambda i: (jnp.minimum((i + 1) * blocks, last), 0)))


def _ffn_specs():
    return [_const_spec((1, D)), _const_spec((D, D_FF)), _const_spec((D, D_FF)), _const_spec((D_FF, D))]


def _l0_tail(x2d, L, gb, gg, conv_w, att, w_out, fg, wg, wu, wd, tm=512):
    T = x2d.shape[0]
    row = lambda i: (i, 0)
    prev_spec, next_spec = _halo_specs(tm, T, CONV_W)
    return pl.pallas_call(
        functools.partial(_l0_tail_kernel, tiles_per_seq=L // tm),
        grid=(T // tm,),
        in_specs=[
            pl.BlockSpec((tm, D), row),
            pl.BlockSpec((tm, CONV_W), row),
            pl.BlockSpec((tm, CONV_W), row),
            prev_spec,
            next_spec,
            _const_spec((3, CONV_W)),
            pl.BlockSpec((tm, HEADS * V_DIM), row),
            _const_spec((D, D)),
        ] + _ffn_specs(),
        out_specs=pl.BlockSpec((tm, D), row),
        out_shape=jax.ShapeDtypeStruct((T, D), F32),
        compiler_params=_params(),
        name="l0_tail",
    )(x2d, gb, gg, gg, gg, conv_w, att, w_out, fg, wg, wu, wd)


def _l1_tail_kernel(x_ref, y_ref, z_ref, x0_ref, bias_ref, wo_ref, fg_ref, wg_ref, wu_ref, wd_ref, o_ref):
    gated = (y_ref[...] + z_ref[...] * bias_ref[...]) * x0_ref[...]
    x1 = x_ref[...] + _dot(gated.astype(BF16), wo_ref[...])
    o_ref[...] = _ffn(x1, fg_ref, wg_ref, wu_ref, wd_ref)


def _l1_tail(x2d, y, z, x0, bias, w_out, fg, wg, wu, wd, tm=512):
    T = x2d.shape[0]
    row = lambda i: (i, 0)
    tile = pl.BlockSpec((tm, D), row)
    return pl.pallas_call(
        _l1_tail_kernel,
        grid=(T // tm,),
        in_specs=[tile, tile, tile, tile, _const_spec((1, D)), _const_spec((D, D))] + _ffn_specs(),
        out_specs=tile,
        out_shape=jax.ShapeDtypeStruct((T, D), F32),
        compiler_params=_params(),
        name="l1_tail",
    )(x2d, y, z, x0, bias, w_out, fg, wg, wu, wd)


def _l1_in_kernel(x_ref, xprev_ref, xnext_ref, g_ref, w_ref, sw_ref, x0_ref, z_ref, *, tiles_per_seq):
    tm = x_ref.shape[0]
    i = pl.program_id(0) % tiles_per_seq
    keep_prev = jnp.where(i == 0, 0.0, 1.0)
    keep_next = jnp.where(i == tiles_per_seq - 1, 0.0, 1.0)
    x_ext = jnp.concatenate([xprev_ref[...] * keep_prev, x_ref[...], xnext_ref[...] * keep_next], axis=0)
    h = _rms(x_ext, g_ref[...]).astype(BF16)
    rows = slice(8, 8 + tm)

    def conv(c):
        cols = slice(D * c, D * (c + 1))
        p = _dot(h, w_ref[:, cols])
        before = pltpu.roll(p, 1, 0)[rows]
        after = pltpu.roll(p, tm + 15, 0)[rows]
        return before * sw_ref[0:1, cols] + p[rows] * sw_ref[1:2, cols] + after * sw_ref[2:3, cols]

    x0_ref[...] = conv(0)
    z_ref[...] = conv(2) * conv(1)


def _l1_in(x2d, L, g, w, short_w, tm=512):
    T = x2d.shape[0]
    row = lambda i: (i, 0)
    prev_spec, next_spec = _halo_specs(tm, T, D)
    return pl.pallas_call(
        functools.partial(_l1_in_kernel, tiles_per_seq=L // tm),
        grid=(T // tm,),
        in_specs=[pl.BlockSpec((tm, D), row), prev_spec, next_spec, _const_spec((1, D)),
                  _const_spec((D, 3 * D)), _const_spec((3, 3 * D))],
        out_specs=[pl.BlockSpec((tm, D), row), pl.BlockSpec((tm, D), row)],
        out_shape=[jax.ShapeDtypeStruct((T, D), F32), jax.ShapeDtypeStruct((T, D), F32)],
        compiler_params=_params(),
        name="l1_in",
    )(x2d, x2d, x2d, g, w, short_w)


def _filter_kernel(band_ref, phase_ref, w1_ref, b1_ref, w2_ref, b2_ref, w3_ref, b3_ref, fr_ref,
                   w4h_ref, w4l_ref, decay_ref, a_ref, o_ref, *, L):
    half = a_ref.shape[1] // 2
    row = lax.broadcasted_iota(jnp.int32, (half, 1), 0)
    groups = []
    for g in range(2):
        rho = row + g * half
        m = (rho >> 3) * FFT_N2 + (rho & 7) + 8 * pl.program_id(0)
        j = m.astype(F32)
        groups.append((m, j / (L - 1.0), (2.0 * math.pi) * j / L))
    lane = lax.broadcasted_iota(jnp.int32, (1, HEAD_PAD), 1)
    first = lane < FILTER_HIDDEN
    t = jnp.where(first, groups[0][1], groups[1][1])
    w = jnp.where(first, groups[0][2], groups[1][2])
    feats = jnp.where((lane & (FILTER_HIDDEN - 1)) == 0, t, jnp.cos(w * band_ref[...] + phase_ref[...]))
    hp = lax.Precision.HIGHEST
    z = jnp.sin(fr_ref[0:1, :] * (jnp.dot(feats, w1_ref[...], precision=hp, preferred_element_type=F32)
                                  + b1_ref[...]))
    z = jnp.sin(fr_ref[1:2, :] * (jnp.dot(z, w2_ref[...], precision=hp, preferred_element_type=F32)
                                  + b2_ref[...]))
    z = jnp.sin(fr_ref[2:3, :] * (jnp.dot(z, w3_ref[...], precision=hp, preferred_element_type=F32)
                                  + b3_ref[...]))
    taps = ([], [])
    for g, (m, tg, _) in enumerate(groups):
        zg = jnp.where(first, z, 0.0) if g == 0 else jnp.where(first, 0.0, z)
        z_hi = zg.astype(BF16)
        z_lo = (zg - z_hi.astype(F32)).astype(BF16)
        h = _dot(z_hi, w4h_ref[...]) + _dot(z_hi, w4l_ref[...]) + _dot(z_lo, w4h_ref[...])
        window = jnp.exp(-tg * decay_ref[...])
        taps[0].append((h[:, :D] * window).astype(BF16))
        taps[1].append((h[:, D:] * window * jnp.where(m == 0, 0.0, 1.0)).astype(BF16))
    for i in range(2):
        x1 = _dot(a_ref[:, :half], taps[i][0]) + _dot(a_ref[:, half:], taps[i][1])
        o_ref[i] = x1.reshape(o_ref.shape[1:])


def _filter_stage1(L, a8, band, phase, w1, b1, w2, b2, w3, b3, freq, w4_hi, w4_lo, decay):
    hid = 2 * FILTER_HIDDEN
    rows_out, rows_in = a8.shape
    return pl.pallas_call(
        functools.partial(_filter_kernel, L=L),
        grid=(FFT_N2 // 8,),
        in_specs=[
            _const_spec((1, HEAD_PAD)), _const_spec((1, HEAD_PAD)),
            _const_spec((HEAD_PAD, hid)), _const_spec((1, hid)),
            _const_spec((hid, hid)), _const_spec((1, hid)),
            _const_spec((hid, hid)), _const_spec((1, hid)),
            _const_spec((3, hid)),
            _const_spec((hid, 2 * D)),
            _const_spec((hid, 2 * D)),
            _const_spec((1, D)),
            _const_spec((rows_out, rows_in)),
        ],
        out_specs=pl.BlockSpec((2, rows_out // 8, 8, D), lambda i: (0, 0, i, 0)),
        out_shape=jax.ShapeDtypeStruct((2, rows_out // 8, FFT_N2, D), F32),
        compiler_params=_params(),
        name="hyena_filter",
    )(band, phase, w1, b1, w2, b2, w3, b3, freq, w4_hi, w4_lo, decay, a8)


def _stage1_kernel(a_ref, x_ref, o_ref):
    k, r, c = x_ref.shape
    x = x_ref[...].reshape(k * r, c).astype(BF16)
    o_ref[...] = _dot(a_ref[...], x).reshape(o_ref.shape)


def _stage1(a8, x, cols=1024):
    B, K, R, _ = x.shape
    M = a8.shape[0] // 8
    return pl.pallas_call(
        _stage1_kernel,
        grid=(B, R // 8, D // cols),
        in_specs=[_const_spec((8 * M, 8 * K)),
                  pl.BlockSpec((None, K, 8, cols), lambda b, j, c: (b, 0, j, c))],
        out_specs=pl.BlockSpec((None, M, 8, cols), lambda b, j, c: (b, 0, j, c)),
        out_shape=jax.ShapeDtypeStruct((B, M, R, D), F32),
        compiler_params=_params(),
        name="fft_stage1",
    )(a8, x)


def _complex_block(m):
    top, bot = m[:FFT_N2], m[FFT_N2:]
    return jnp.concatenate([m, jnp.concatenate([-bot, top], axis=0)], axis=1)


def _filter_spectrum_kernel(mf_ref, x_ref, o_ref):
    g = _complex_block(mf_ref[...])
    zf = _dot(g, x_ref[0].reshape(2 * FFT_N2, D).astype(BF16))
    zb = _dot(g, x_ref[1].reshape(2 * FFT_N2, D).astype(BF16))
    o_ref[0] = (zf[:FFT_N2] + zb[:FFT_N2]).astype(o_ref.dtype)
    o_ref[1] = (zf[FFT_N2:] - zb[FFT_N2:]).astype(o_ref.dtype)


def _filter_spectrum(mf, x1):
    H = x1.shape[2]
    return pl.pallas_call(
        _filter_spectrum_kernel,
        grid=(H,),
        in_specs=[pl.BlockSpec((None, 2 * FFT_N2, FFT_N2), lambda k: (k, 0, 0)),
                  pl.BlockSpec((2, 2, None, FFT_N2, D), lambda k: (0, 0, k, 0, 0))],
        out_specs=pl.BlockSpec((2, None, FFT_N2, D), lambda k: (0, k, 0, 0)),
        out_shape=jax.ShapeDtypeStruct((2, H, FFT_N2, D), BF16),
        compiler_params=_params(),
        name="fft_filter_spectrum",
    )(mf, x1)


def _fft_mid_kernel(mf_ref, kf_ref, x_ref, o_ref):
    g = _complex_block(mf_ref[...])
    x = x_ref[...].reshape(2 * FFT_N2, D).astype(BF16)
    zf = _dot(g, x)
    zr, zi = zf[:FFT_N2], zf[FFT_N2:]
    kr, ki = kf_ref[0].astype(F32), kf_ref[1].astype(F32)
    y = jnp.concatenate([zr * kr - zi * ki, zr * ki + zi * kr], axis=0).astype(BF16)
    yi = lax.dot_general(g, y, (((0,), (0,)), ((), ())), preferred_element_type=F32)
    o_ref[...] = yi.reshape(2, FFT_N2, D)


def _fft_mid(mf, kf, x1):
    B, _, H = x1.shape[:3]
    mat = pl.BlockSpec((None, 2 * FFT_N2, FFT_N2), lambda k, b: (k, 0, 0))
    return pl.pallas_call(
        _fft_mid_kernel,
        grid=(H, B),
        in_specs=[mat,
                  pl.BlockSpec((2, None, FFT_N2, D), lambda k, b: (0, k, 0, 0)),
                  pl.BlockSpec((None, 2, None, FFT_N2, D), lambda k, b: (b, 0, k, 0, 0))],
        out_specs=pl.BlockSpec((None, 2, None, FFT_N2, D), lambda k, b: (b, 0, k, 0, 0)),
        out_shape=jax.ShapeDtypeStruct(x1.shape, F32),
        compiler_params=_params(),
        name="fft_mid",
    )(mf, kf, x1)


@functools.lru_cache(maxsize=None)
def _fft_constants(L):
    N = 2 * L
    N1 = N // FFT_N2
    H = N1 // 2
    k1 = np.arange(H, dtype=np.float64) + 0.5
    th = 2.0 * np.pi * np.outer(k1, np.arange(N1, dtype=np.float64)) / N1
    a_full = np.concatenate([np.cos(th), -np.sin(th)], axis=0)
    a_half = a_full[:, :H]
    thi = th[:, :H].T
    b_inv = (2.0 / N) * np.concatenate([np.cos(thi), -np.sin(thi)], axis=1)
    n2 = np.arange(FFT_N2, dtype=np.float64)
    freq = k1[:, None, None] + N1 * n2[None, :, None]
    ang = 2.0 * np.pi * (freq * n2[None, None, :] % N) / N
    mf = np.concatenate([np.cos(ang), -np.sin(ang)], axis=1)
    eye8 = np.eye(8)
    mats = (np.kron(a_half, eye8), np.kron(b_inv, eye8), mf)
    return tuple(v.astype(np.float32) for v in mats)


def _fft_conv(z, filter_params, B, L):
    H = L // FFT_N2
    a_half8, b_inv8, mf = (jnp.asarray(v).astype(BF16) for v in _fft_constants(L))
    kf = _filter_spectrum(mf, _filter_stage1(L, a_half8, *filter_params).reshape(2, 2, H, FFT_N2, D))
    x1 = _stage1(a_half8, z.reshape(B, H, FFT_N2, D)).reshape(B, 2, H, FFT_N2, D)
    y1 = _fft_mid(mf, kf, x1).reshape(B, 2 * H, FFT_N2, D)
    return _stage1(b_inv8, y1).reshape(B * L, D)


def _rope_tables(L):
    half = QK_ROPE // 2
    inv = ROPE_THETA ** (-jnp.arange(half, dtype=F32) / half)
    ang = jnp.arange(L, dtype=F32)[:, None] * inv[None, :]
    cos, sin = jnp.cos(ang), jnp.sin(ang)
    ones = jnp.ones((L, QK_NOPE), F32)
    zeros64 = jnp.zeros((L, QK_NOPE), F32)
    pad = jnp.zeros((L, HEAD_PAD - QK_DIM), F32)
    cos_t = jnp.concatenate([ones, cos, cos, pad], axis=1)
    sin_t = jnp.concatenate([zeros64, sin, sin, pad], axis=1)
    return cos_t, sin_t


def _rotary_partner(w):
    k = w.shape[0]
    half = QK_ROPE // 2
    w = w.reshape(k, HEADS, HEAD_PAD)
    first, second = w[:, :, QK_NOPE:QK_NOPE + half], w[:, :, QK_NOPE + half:QK_DIM]
    out = jnp.concatenate([jnp.zeros((k, HEADS, QK_NOPE), w.dtype), -second, first,
                           jnp.zeros((k, HEADS, HEAD_PAD - QK_DIM), w.dtype)], axis=2)
    return out.reshape(k, HEADS * HEAD_PAD)


def _pad_heads(w, width):
    k = w.shape[0]
    w = w.reshape(k, HEADS, width)
    return jnp.pad(w, ((0, 0), (0, 0), (0, HEAD_PAD - width))).reshape(k, HEADS * HEAD_PAD)


def _prepare_weights(mix_norm, ffn_norm, ffn_w_gate, ffn_w_up, ffn_w_down,
                     a_w_in, a_conv_w, a_q_a_norm, a_kv_a_norm, a_w_q_up, a_w_kv_up, a_q_norm, a_k_norm,
                     a_w_out, c_w_in, c_short_w, c_f_w1, c_f_b1, c_f_w2, c_f_b2, c_f_w3, c_f_b3, c_f_freq,
                     c_f_w4, c_bias, c_w_out):
    p = {}
    p["mix_g"] = [mix_norm[i].reshape(1, D) for i in range(2)]
    p["ffn"] = [(ffn_norm[i].reshape(1, D), ffn_w_gate[i].astype(BF16), ffn_w_up[i].astype(BF16),
                 ffn_w_down[i].astype(BF16)) for i in range(2)]
    p["a_w_in"] = jnp.pad(a_w_in[0], ((0, 0), (0, IN_A_PAD - a_w_in.shape[2]))).astype(BF16)
    p["a_conv_w"] = a_conv_w[0].T
    p["qan"] = a_q_a_norm[0].reshape(1, Q_LORA)
    p["kvan"] = a_kv_a_norm[0].reshape(1, KV_LORA)
    wq = _pad_heads(a_w_q_up[0], QK_DIM)
    p["wq"], p["wq_rot"] = wq.astype(BF16), _rotary_partner(wq).astype(BF16)
    kv = a_w_kv_up[0].reshape(KV_LORA, HEADS, QK_NOPE + V_DIM)
    p["wk"] = _pad_heads(kv[:, :, :QK_NOPE].reshape(KV_LORA, HEADS * QK_NOPE), QK_NOPE).astype(BF16)
    p["wv"] = _pad_heads(kv[:, :, QK_NOPE:].reshape(KV_LORA, HEADS * V_DIM), V_DIM).astype(BF16)
    v_ones = np.zeros((1, HEADS, HEAD_PAD), np.float32)
    v_ones[:, :, V_DIM:] = 1.0
    p["v_ones"] = jnp.asarray(v_ones.reshape(1, HEADS * HEAD_PAD))
    place = np.zeros((HEAD_PAD, HEADS, HEAD_PAD), np.float32)
    for r in range(QK_ROPE):
        place[r, :, QK_NOPE + r] = 1.0
    wpe = jnp.asarray(place.reshape(HEAD_PAD, HEADS * HEAD_PAD))
    p["wpe"], p["wpe_rot"] = wpe.astype(BF16), _rotary_partner(wpe).astype(BF16)
    half = QK_ROPE // 2
    for name, gain in (("qg", a_q_norm[0]), ("kg", a_k_norm[0])):
        p[name] = jnp.pad(gain, (0, HEAD_PAD - QK_DIM)).reshape(1, HEAD_PAD)
        swapped = jnp.concatenate([jnp.zeros((QK_NOPE,), F32), gain[QK_NOPE + half:], gain[QK_NOPE:QK_NOPE + half]])
        p[name + "_rot"] = jnp.pad(swapped, (0, HEAD_PAD - QK_DIM)).reshape(1, HEAD_PAD)
    p["ones"] = jnp.ones((HEAD_PAD, HEAD_PAD), BF16)
    p["a_w_out"] = a_w_out[0].astype(BF16)
    p["c_w_in"] = c_w_in[0].astype(BF16)
    p["c_short_w"] = c_short_w[0].T
    hid = FILTER_HIDDEN
    blockdiag = lambda w: jnp.kron(jnp.eye(2, dtype=F32), w)
    p["f_w1"] = blockdiag(jnp.pad(c_f_w1[0], ((0, hid - FILTER_EMB), (0, 0))))
    p["f_w2"], p["f_w3"] = blockdiag(c_f_w2[0]), blockdiag(c_f_w3[0])
    p["f_b"] = [jnp.tile(b[0].reshape(1, hid), (1, 2)) for b in (c_f_b1, c_f_b2, c_f_b3)]
    p["f_freq"] = jnp.tile(c_f_freq[0], (1, 2))
    w4 = jnp.tile(c_f_w4[0], (2, 1))
    p["f_w4_hi"] = w4.astype(BF16)
    p["f_w4_lo"] = (w4 - p["f_w4_hi"].astype(F32)).astype(BF16)
    p["c_bias"] = c_bias[0].reshape(1, D)
    p["c_w_out"] = c_w_out[0].astype(BF16)
    bands = np.linspace(1e-4, FILTER_BANDS - 1, FILTER_BANDS, dtype=np.float32)
    band = np.zeros((1, hid), np.float32)
    phase = np.zeros((1, hid), np.float32)
    band[0, 1:1 + FILTER_BANDS] = bands
    band[0, 1 + FILTER_BANDS:FILTER_EMB] = bands
    phase[0, 1 + FILTER_BANDS:FILTER_EMB] = np.pi / 2
    p["band"], p["phase"] = jnp.asarray(np.tile(band, (1, 2))), jnp.asarray(np.tile(phase, (1, 2)))
    max_decay = math.log(1e-2) / 0.3
    min_decay = math.log(1e-2) / 1.5
    p["decay"] = jnp.asarray(np.abs(np.linspace(min_decay, max_decay, D, dtype=np.float32)).reshape(1, D))
    return p


def _trunk(x, p):
    B, L, _ = x.shape
    x2d = x.reshape(B * L, D)
    cos_t, sin_t = _rope_tables(L)
    gb, gg, q, k, v = _l0_in(x2d, L, p, cos_t, sin_t)
    att = _flash(q, k, v, B, L)
    x2d = _l0_tail(x2d, L, gb, gg, p["a_conv_w"], att, p["a_w_out"], *p["ffn"][0])
    x0, z = _l1_in(x2d, L, p["mix_g"][1], p["c_w_in"], p["c_short_w"])
    filter_params = (p["band"], p["phase"], p["f_w1"], p["f_b"][0], p["f_w2"], p["f_b"][1], p["f_w3"],
                     p["f_b"][2], p["f_freq"], p["f_w4_hi"], p["f_w4_lo"], p["decay"])
    y = _fft_conv(z, filter_params, B, L)
    x2d = _l1_tail(x2d, y, z, x0, p["c_bias"], p["c_w_out"], *p["ffn"][1])
    return x2d.reshape(B, L, D)


def kernel(x_prompt, x_sample, mix_norm, ffn_norm, ffn_w_gate, ffn_w_up, ffn_w_down, a_w_in, a_conv_w, a_q_a_norm, a_kv_a_norm, a_w_q_up, a_w_kv_up, a_q_norm, a_k_norm, a_w_out, c_w_in, c_short_w, c_f_w1, c_f_b1, c_f_w2, c_f_b2, c_f_w3, c_f_b3, c_f_freq, c_f_w4, c_bias, c_w_out):
    p = _prepare_weights(mix_norm, ffn_norm, ffn_w_gate, ffn_w_up, ffn_w_down, a_w_in, a_conv_w,
                         a_q_a_norm, a_kv_a_norm, a_w_q_up, a_w_kv_up, a_q_norm, a_k_norm, a_w_out,
                         c_w_in, c_short_w, c_f_w1, c_f_b1, c_f_w2, c_f_b2, c_f_w3, c_f_b3, c_f_freq,
                         c_f_w4, c_bias, c_w_out)
    return (_trunk(x_prompt, p), _trunk(x_sample, p))
```

```python
import functools
import math

import jax
import jax.numpy as jnp
import numpy as np
from jax import lax
from jax.experimental import pallas as pl
from jax.experimental.pallas import tpu as pltpu

F32 = jnp.float32
BF16 = jnp.bfloat16

D = 1024
EPS = 1e-6
CONV_W = 512
HEADS = 8
QK_NOPE = 64
QK_ROPE = 32
V_DIM = 64
QK_DIM = 96
Q_LORA = 256
KV_LORA = 128
ROPE_THETA = 10000.0
LOG2E = 1.4426950408889634
HEAD_PAD = 128
IN_A_PAD = 2048
FILTER_BANDS = 16
FILTER_EMB = 33
FILTER_HIDDEN = 64
D_FF = 2816
MXU_DIM = 256
FFN_SPLITS = (0, (D_FF // MXU_DIM + 1) // 2 * MXU_DIM, D_FF)
FFT_N2 = 256

VMEM_LIMIT_BYTES = 56 * 1024 * 1024


def _params():
    return pltpu.CompilerParams(vmem_limit_bytes=VMEM_LIMIT_BYTES)


def _const_spec(shape):
    zeros = (0,) * len(shape)
    return pl.BlockSpec(shape, lambda *_: zeros, pipeline_mode=pl.Buffered(1))


def _dot(a, b):
    return jnp.dot(a, b, preferred_element_type=F32)


def _rms(x, g):
    ms = jnp.mean(x * x, axis=-1, keepdims=True)
    return x * lax.rsqrt(ms + EPS) * g


def _l0_in_kernel(x_ref, g_ref, w_ref, qan_ref, kvan_ref, wq_ref, wqr_ref, wk_ref, wpe_ref, wper_ref,
                  wv_ref, vone_ref, qg_ref, qgr_ref, kg_ref, kgr_ref, ones_ref, cos_ref, sin_ref,
                  gb_ref, gg_ref, q_ref, k_ref, v_ref):
    h = _rms(x_ref[...], g_ref[...]).astype(BF16)
    proj = _dot(h, w_ref[...])
    gb_ref[...] = proj[:, 0:CONV_W]
    gg_ref[...] = proj[:, CONV_W:2 * CONV_W] * proj[:, 2 * CONV_W:3 * CONV_W]
    c0 = 3 * CONV_W
    qn = _rms(proj[:, c0:c0 + Q_LORA], qan_ref[...]).astype(BF16)
    kvn = _rms(proj[:, c0 + Q_LORA:c0 + Q_LORA + KV_LORA], kvan_ref[...]).astype(BF16)
    pe = proj[:, c0 + Q_LORA + KV_LORA:IN_A_PAD].astype(BF16)
    qf, qr = _dot(qn, wq_ref[...]), _dot(qn, wqr_ref[...])
    kf, kr = _dot(kvn, wk_ref[...]) + _dot(pe, wpe_ref[...]), _dot(pe, wper_ref[...])
    v_ref[...] = (_dot(kvn, wv_ref[...]) + vone_ref[...]).astype(BF16)
    cos, sin = cos_ref[...], sin_ref[...]
    ones = ones_ref[...]
    sides = ((qf, qr, cos * qg_ref[...], sin * qgr_ref[...], q_ref, QK_DIM ** -0.5 * LOG2E),
             (kf, kr, cos * kg_ref[...], sin * kgr_ref[...], k_ref, 1.0))
    for hd in range(HEADS):
        sl = slice(HEAD_PAD * hd, HEAD_PAD * (hd + 1))
        for full, rot, cg, sg, out_ref, mult in sides:
            xh = full[:, sl]
            ss = _dot((xh * xh).astype(BF16), ones) * (1.0 / QK_DIM)
            out_ref[:, sl] = (lax.rsqrt(ss + EPS) * mult * (xh * cg + rot[:, sl] * sg)).astype(BF16)


def _l0_in(x2d, L, p, cos_t, sin_t, tm=512):
    T = x2d.shape[0]
    tps = L // tm
    row = lambda i: (i, 0)
    pos = lambda i: (i % tps, 0)
    wide = HEADS * HEAD_PAD
    return pl.pallas_call(
        _l0_in_kernel,
        grid=(T // tm,),
        in_specs=[
            pl.BlockSpec((tm, D), row),
            _const_spec((1, D)),
            _const_spec((D, IN_A_PAD)),
            _const_spec((1, Q_LORA)),
            _const_spec((1, KV_LORA)),
            _const_spec((Q_LORA, wide)),
            _const_spec((Q_LORA, wide)),
            _const_spec((KV_LORA, wide)),
            _const_spec((HEAD_PAD, wide)),
            _const_spec((HEAD_PAD, wide)),
            _const_spec((KV_LORA, wide)),
            _const_spec((1, wide)),
            _const_spec((1, HEAD_PAD)),
            _const_spec((1, HEAD_PAD)),
            _const_spec((1, HEAD_PAD)),
            _const_spec((1, HEAD_PAD)),
            _const_spec((HEAD_PAD, HEAD_PAD)),
            pl.BlockSpec((tm, HEAD_PAD), pos),
            pl.BlockSpec((tm, HEAD_PAD), pos),
        ],
        out_specs=[
            pl.BlockSpec((tm, CONV_W), row),
            pl.BlockSpec((tm, CONV_W), row),
            pl.BlockSpec((tm, wide), row),
            pl.BlockSpec((tm, wide), row),
            pl.BlockSpec((tm, wide), row),
        ],
        out_shape=[
            jax.ShapeDtypeStruct((T, CONV_W), F32),
            jax.ShapeDtypeStruct((T, CONV_W), F32),
            jax.ShapeDtypeStruct((T, wide), BF16),
            jax.ShapeDtypeStruct((T, wide), BF16),
            jax.ShapeDtypeStruct((T, wide), BF16),
        ],
        compiler_params=_params(),
        name="l0_in",
    )(x2d, p["mix_g"][0], p["a_w_in"], p["qan"], p["kvan"], p["wq"], p["wq_rot"], p["wk"], p["wpe"],
      p["wpe_rot"], p["wv"], p["v_ones"], p["qg"], p["qg_rot"], p["kg"], p["kg_rot"], p["ones"],
      cos_t, sin_t)


def _flash_kernel(q_ref, k_ref, v_ref, o_ref, m_ref, acc_ref, *, nk):
    kv = pl.program_id(2)
    tq = q_ref.shape[0]

    @pl.when(kv == 0)
    def _():
        m_ref[...] = jnp.full(m_ref.shape, -jnp.inf, F32)
        acc_ref[...] = jnp.zeros(acc_ref.shape, F32)

    for hd in range(HEADS):
        sl = slice(HEAD_PAD * hd, HEAD_PAD * (hd + 1))
        s = lax.dot_general(q_ref[:, sl], k_ref[:, sl], (((1,), (1,)), ((), ())),
                            preferred_element_type=F32)
        m_prev = m_ref[hd]
        m_new = jnp.maximum(m_prev, jnp.max(s, axis=-1, keepdims=True))
        p = jnp.exp2((s - m_new[:, 0:1]).astype(BF16))
        acc_ref[:, sl] = acc_ref[:, sl] * jnp.exp2(m_prev - m_new) + _dot(p, v_ref[:, sl])
        m_ref[hd] = m_new

    @pl.when(kv == nk - 1)
    def _():
        low_half = lax.broadcasted_iota(jnp.int32, (tq, HEAD_PAD), 1) < V_DIM
        for pair in range(HEADS // 2):
            a = acc_ref[:, HEAD_PAD * 2 * pair:HEAD_PAD * (2 * pair + 1)]
            b = acc_ref[:, HEAD_PAD * (2 * pair + 1):HEAD_PAD * (2 * pair + 2)]
            o_ref[:, HEAD_PAD * pair:HEAD_PAD * (pair + 1)] = jnp.where(
                low_half, a / pltpu.roll(a, V_DIM, 1), pltpu.roll(b, V_DIM, 1) / b).astype(o_ref.dtype)


def _flash(q, k, v, B, L, tq=2048, tk=512):
    tq = min(tq, L)
    tk = min(tk, L)
    nq, nk = L // tq, L // tk
    wide = HEADS * HEAD_PAD
    return pl.pallas_call(
        functools.partial(_flash_kernel, nk=nk),
        grid=(B, nq, nk),
        in_specs=[
            pl.BlockSpec((tq, wide), lambda b, i, j: (b * nq + i, 0)),
            pl.BlockSpec((tk, wide), lambda b, i, j: (b * nk + j, 0)),
            pl.BlockSpec((tk, wide), lambda b, i, j: (b * nk + j, 0)),
        ],
        out_specs=pl.BlockSpec((tq, HEADS * V_DIM), lambda b, i, j: (b * nq + i, 0)),
        out_shape=jax.ShapeDtypeStruct((B * L, HEADS * V_DIM), BF16),
        scratch_shapes=[
            pltpu.VMEM((HEADS, tq, HEAD_PAD), F32),
            pltpu.VMEM((tq, wide), F32),
        ],
        compiler_params=_params(),
        name="flash_attention",
    )(q, k, v)


def _ffn(x1, fg_ref, wg_ref, wu_ref, wd_ref):
    hn = _rms(x1, fg_ref[...]).astype(BF16)
    out = x1
    for lo, hi in zip(FFN_SPLITS[:-1], FFN_SPLITS[1:]):
        cols = slice(lo, hi)
        g = _dot(hn, wg_ref[:, cols])
        u = _dot(hn, wu_ref[:, cols])
        a = (g * (1.0 / (1.0 + jnp.exp(-g))) * u).astype(BF16)
        out = out + _dot(a, wd_ref[cols, :])
    return out


def _shift_rows(cur, prev_row, next_row):
    tm = cur.shape[0]
    row = lax.broadcasted_iota(jnp.int32, cur.shape, 0)
    before = jnp.where(row == 0, prev_row, pltpu.roll(cur, 1, 0))
    after = jnp.where(row == tm - 1, next_row, pltpu.roll(cur, tm - 1, 0))
    return before, after


def _halo_rows(prev_ref, next_ref, cols, tiles_per_seq):
    i = pl.program_id(0) % tiles_per_seq
    prev_row = jnp.where(i == 0, 0.0, prev_ref[7:8, cols])
    next_row = jnp.where(i == tiles_per_seq - 1, 0.0, next_ref[0:1, cols])
    return prev_row, next_row


def _l0_tail_kernel(x_ref, gb_ref, gg_ref, gprev_ref, gnext_ref, cw_ref, att_ref, wo_ref,
                    fg_ref, wg_ref, wu_ref, wd_ref, o_ref, *, tiles_per_seq):
    gg = gg_ref[...]
    prev_row, next_row = _halo_rows(gprev_ref, gnext_ref, slice(None), tiles_per_seq)
    before, after = _shift_rows(gg, prev_row, next_row)
    cw = cw_ref[...]
    y_conv = gb_ref[...] * (before * cw[0:1] + gg * cw[1:2] + after * cw[2:3])
    mix = _dot(y_conv.astype(BF16), wo_ref[0:CONV_W, :]) + _dot(att_ref[...], wo_ref[CONV_W:, :])
    o_ref[...] = _ffn(x_ref[...] + mix, fg_ref, wg_ref, wu_ref, wd_ref)


def _halo_specs(tm, n_rows, width):
    blocks = tm // 8
    last = n_rows // 8 - 1
    return (pl.BlockSpec((8, width), lambda i: (jnp.maximum(i * blocks - 1, 0), 0)),
            pl.BlockSpec((8, width), lambda i: (jnp.minimum((i + 1) * blocks, last), 0)))


def _ffn_specs():
    return [_const_spec((1, D)), _const_spec((D, D_FF)), _const_spec((D, D_FF)), _const_spec((D_FF, D))]


def _l0_tail(x2d, L, gb, gg, conv_w, att, w_out, fg, wg, wu, wd, tm=512):
    T = x2d.shape[0]
    row = lambda i: (i, 0)
    prev_spec, next_spec = _halo_specs(tm, T, CONV_W)
    return pl.pallas_call(
        functools.partial(_l0_tail_kernel, tiles_per_seq=L // tm),
        grid=(T // tm,),
        in_specs=[
            pl.BlockSpec((tm, D), row),
            pl.BlockSpec((tm, CONV_W), row),
            pl.BlockSpec((tm, CONV_W), row),
            prev_spec,
            next_spec,
            _const_spec((3, CONV_W)),
            pl.BlockSpec((tm, HEADS * V_DIM), row),
            _const_spec((D, D)),
        ] + _ffn_specs(),
        out_specs=pl.BlockSpec((tm, D), row),
        out_shape=jax.ShapeDtypeStruct((T, D), F32),
        compiler_params=_params(),
        name="l0_tail",
    )(x2d, gb, gg, gg, gg, conv_w, att, w_out, fg, wg, wu, wd)


def _l1_tail_kernel(x_ref, y_ref, z_ref, x0_ref, bias_ref, wo_ref, fg_ref, wg_ref, wu_ref, wd_ref, o_ref):
    gated = (y_ref[...] + z_ref[...] * bias_ref[...]) * x0_ref[...]
    x1 = x_ref[...] + _dot(gated.astype(BF16), wo_ref[...])
    o_ref[...] = _ffn(x1, fg_ref, wg_ref, wu_ref, wd_ref)


def _l1_tail(x2d, y, z, x0, bias, w_out, fg, wg, wu, wd, tm=512):
    T = x2d.shape[0]
    row = lambda i: (i, 0)
    tile = pl.BlockSpec((tm, D), row)
    return pl.pallas_call(
        _l1_tail_kernel,
        grid=(T // tm,),
        in_specs=[tile, tile, tile, tile, _const_spec((1, D)), _const_spec((D, D))] + _ffn_specs(),
        out_specs=tile,
        out_shape=jax.ShapeDtypeStruct((T, D), F32),
        compiler_params=_params(),
        name="l1_tail",
    )(x2d, y, z, x0, bias, w_out, fg, wg, wu, wd)


def _l1_in_kernel(x_ref, xprev_ref, xnext_ref, g_ref, w_ref, sw_ref, x0_ref, z_ref, *, tiles_per_seq):
    tm = x_ref.shape[0]
    i = pl.program_id(0) % tiles_per_seq
    keep_prev = jnp.where(i == 0, 0.0, 1.0)
    keep_next = jnp.where(i == tiles_per_seq - 1, 0.0, 1.0)
    x_ext = jnp.concatenate([xprev_ref[...] * keep_prev, x_ref[...], xnext_ref[...] * keep_next], axis=0)
    h = _rms(x_ext, g_ref[...]).astype(BF16)
    rows = slice(8, 8 + tm)

    def conv(c):
        cols = slice(D * c, D * (c + 1))
        p = _dot(h, w_ref[:, cols])
        before = pltpu.roll(p, 1, 0)[rows]
        after = pltpu.roll(p, tm + 15, 0)[rows]
        return before * sw_ref[0:1, cols] + p[rows] * sw_ref[1:2, cols] + after * sw_ref[2:3, cols]

    x0_ref[...] = conv(0)
    z_ref[...] = conv(2) * conv(1)


def _l1_in(x2d, L, g, w, short_w, tm=512):
    T = x2d.shape[0]
    row = lambda i: (i, 0)
    prev_spec, next_spec = _halo_specs(tm, T, D)
    return pl.pallas_call(
        functools.partial(_l1_in_kernel, tiles_per_seq=L // tm),
        grid=(T // tm,),
        in_specs=[pl.BlockSpec((tm, D), row), prev_spec, next_spec, _const_spec((1, D)),
                  _const_spec((D, 3 * D)), _const_spec((3, 3 * D))],
        out_specs=[pl.BlockSpec((tm, D), row), pl.BlockSpec((tm, D), row)],
        out_shape=[jax.ShapeDtypeStruct((T, D), F32), jax.ShapeDtypeStruct((T, D), F32)],
        compiler_params=_params(),
        name="l1_in",
    )(x2d, x2d, x2d, g, w, short_w)


def _filter_kernel(band_ref, phase_ref, w1_ref, b1_ref, w2_ref, b2_ref, w3_ref, b3_ref, fr_ref,
                   w4h_ref, w4l_ref, decay_ref, a_ref, o_ref, *, L):
    half = a_ref.shape[1] // 2
    row = lax.broadcasted_iota(jnp.int32, (half, 1), 0)
    groups = []
    for g in range(2):
        rho = row + g * half
        m = (rho >> 3) * FFT_N2 + (rho & 7) + 8 * pl.program_id(0)
        j = m.astype(F32)
        groups.append((m, j / (L - 1.0), (2.0 * math.pi) * j / L))
    lane = lax.broadcasted_iota(jnp.int32, (1, HEAD_PAD), 1)
    first = lane < FILTER_HIDDEN
    t = jnp.where(first, groups[0][1], groups[1][1])
    w = jnp.where(first, groups[0][2], groups[1][2])
    feats = jnp.where((lane & (FILTER_HIDDEN - 1)) == 0, t, jnp.cos(w * band_ref[...] + phase_ref[...]))
    hp = lax.Precision.HIGHEST
    z = jnp.sin(fr_ref[0:1, :] * (jnp.dot(feats, w1_ref[...], precision=hp, preferred_element_type=F32)
                                  + b1_ref[...]))
    z = jnp.sin(fr_ref[1:2, :] * (jnp.dot(z, w2_ref[...], precision=hp, preferred_element_type=F32)
                                  + b2_ref[...]))
    z = jnp.sin(fr_ref[2:3, :] * (jnp.dot(z, w3_ref[...], precision=hp, preferred_element_type=F32)
                                  + b3_ref[...]))
    taps = ([], [])
    for g, (m, tg, _) in enumerate(groups):
        zg = jnp.where(first, z, 0.0) if g == 0 else jnp.where(first, 0.0, z)
        z_hi = zg.astype(BF16)
        z_lo = (zg - z_hi.astype(F32)).astype(BF16)
        h = _dot(z_hi, w4h_ref[...]) + _dot(z_hi, w4l_ref[...]) + _dot(z_lo, w4h_ref[...])
        window = jnp.exp(-tg * decay_ref[...])
        taps[0].append((h[:, :D] * window).astype(BF16))
        taps[1].append((h[:, D:] * window * jnp.where(m == 0, 0.0, 1.0)).astype(BF16))
    for i in range(2):
        x1 = _dot(a_ref[:, :half], taps[i][0]) + _dot(a_ref[:, half:], taps[i][1])
        o_ref[i] = x1.reshape(o_ref.shape[1:])


def _filter_stage1(L, a8, band, phase, w1, b1, w2, b2, w3, b3, freq, w4_hi, w4_lo, decay):
    hid = 2 * FILTER_HIDDEN
    rows_out, rows_in = a8.shape
    return pl.pallas_call(
        functools.partial(_filter_kernel, L=L),
        grid=(FFT_N2 // 8,),
        in_specs=[
            _const_spec((1, HEAD_PAD)), _const_spec((1, HEAD_PAD)),
            _const_spec((HEAD_PAD, hid)), _const_spec((1, hid)),
            _const_spec((hid, hid)), _const_spec((1, hid)),
            _const_spec((hid, hid)), _const_spec((1, hid)),
            _const_spec((3, hid)),
            _const_spec((hid, 2 * D)),
            _const_spec((hid, 2 * D)),
            _const_spec((1, D)),
            _const_spec((rows_out, rows_in)),
        ],
        out_specs=pl.BlockSpec((2, rows_out // 8, 8, D), lambda i: (0, 0, i, 0)),
        out_shape=jax.ShapeDtypeStruct((2, rows_out // 8, FFT_N2, D), F32),
        compiler_params=_params(),
        name="hyena_filter",
    )(band, phase, w1, b1, w2, b2, w3, b3, freq, w4_hi, w4_lo, decay, a8)


def _stage1_kernel(a_ref, x_ref, o_ref):
    k, r, c = x_ref.shape
    a = a_ref[...]
    for i in range(r // 8):
        rows = slice(8 * i, 8 * (i + 1))
        x = x_ref[:, rows, :].reshape(k * 8, c).astype(BF16)
        o_ref[:, rows, :] = _dot(a, x).reshape(o_ref.shape[0], 8, c)


def _stage1(a8, x, cols=1024, rows=16):
    B, K, R, _ = x.shape
    M = a8.shape[0] // 8
    return pl.pallas_call(
        _stage1_kernel,
        grid=(B, R // rows, D // cols),
        in_specs=[_const_spec((8 * M, 8 * K)),
                  pl.BlockSpec((None, K, rows, cols), lambda b, j, c: (b, 0, j, c))],
        out_specs=pl.BlockSpec((None, M, rows, cols), lambda b, j, c: (b, 0, j, c)),
        out_shape=jax.ShapeDtypeStruct((B, M, R, D), F32),
        compiler_params=_params(),
        name="fft_stage1",
    )(a8, x)


def _complex_block(m):
    top, bot = m[:FFT_N2], m[FFT_N2:]
    return jnp.concatenate([m, jnp.concatenate([-bot, top], axis=0)], axis=1)


def _fft_mid_kernel(mf_ref, f_ref, x_ref, o_ref, kf_ref):
    g = _complex_block(mf_ref[...].astype(BF16))

    @pl.when(pl.program_id(1) == 0)
    def _():
        zf = _dot(g, f_ref[0].reshape(2 * FFT_N2, D).astype(BF16))
        zb = _dot(g, f_ref[1].reshape(2 * FFT_N2, D).astype(BF16))
        kf_ref[0] = zf[:FFT_N2] + zb[:FFT_N2]
        kf_ref[1] = zf[FFT_N2:] - zb[FFT_N2:]

    x = x_ref[...].reshape(2 * FFT_N2, D).astype(BF16)
    zf = _dot(g, x)
    zr, zi = zf[:FFT_N2], zf[FFT_N2:]
    kr, ki = kf_ref[0], kf_ref[1]
    y = jnp.concatenate([zr * kr - zi * ki, zr * ki + zi * kr], axis=0).astype(BF16)
    yi = lax.dot_general(g, y, (((0,), (0,)), ((), ())), preferred_element_type=F32)
    o_ref[...] = yi.reshape(2, FFT_N2, D)


def _fft_mid(mf, f1, x1):
    B, _, H = x1.shape[:3]
    return pl.pallas_call(
        _fft_mid_kernel,
        grid=(H, B),
        in_specs=[pl.BlockSpec((None, 2 * FFT_N2, FFT_N2), lambda k, b: (k, 0, 0)),
                  pl.BlockSpec((2, 2, None, FFT_N2, D), lambda k, b: (0, 0, k, 0, 0)),
                  pl.BlockSpec((None, 2, None, FFT_N2, D), lambda k, b: (b, 0, k, 0, 0))],
        out_specs=pl.BlockSpec((None, 2, None, FFT_N2, D), lambda k, b: (b, 0, k, 0, 0)),
        out_shape=jax.ShapeDtypeStruct(x1.shape, F32),
        scratch_shapes=[pltpu.VMEM((2, FFT_N2, D), F32)],
        compiler_params=_params(),
        name="fft_mid",
    )(mf, f1, x1)


@functools.lru_cache(maxsize=None)
def _fft_constants(L):
    N = 2 * L
    N1 = N // FFT_N2
    H = N1 // 2
    k1 = np.arange(H, dtype=np.float64) + 0.5
    th = 2.0 * np.pi * np.outer(k1, np.arange(N1, dtype=np.float64)) / N1
    a_full = np.concatenate([np.cos(th), -np.sin(th)], axis=0)
    a_half = a_full[:, :H]
    thi = th[:, :H].T
    b_inv = (2.0 / N) * np.concatenate([np.cos(thi), -np.sin(thi)], axis=1)
    n2 = np.arange(FFT_N2, dtype=np.float64)
    freq = k1[:, None, None] + N1 * n2[None, :, None]
    ang = 2.0 * np.pi * (freq * n2[None, None, :] % N) / N
    mf = np.concatenate([np.cos(ang), -np.sin(ang)], axis=1)
    eye8 = np.eye(8)
    mats = (np.kron(a_half, eye8), np.kron(b_inv, eye8), mf)
    return tuple(v.astype(np.float32) for v in mats)


def _fft_conv(z, filter_params, B, L):
    H = L // FFT_N2
    a_half8, b_inv8, mf = _fft_constants(L)
    a_half8, b_inv8 = jnp.asarray(a_half8).astype(BF16), jnp.asarray(b_inv8).astype(BF16)
    f1 = _filter_stage1(L, a_half8, *filter_params).reshape(2, 2, H, FFT_N2, D)
    x1 = _stage1(a_half8, z.reshape(B, H, FFT_N2, D)).reshape(B, 2, H, FFT_N2, D)
    y1 = _fft_mid(jnp.asarray(mf), f1, x1).reshape(B, 2 * H, FFT_N2, D)
    return _stage1(b_inv8, y1).reshape(B * L, D)


def _rope_tables(L):
    half = QK_ROPE // 2
    inv = ROPE_THETA ** (-jnp.arange(half, dtype=F32) / half)
    ang = jnp.arange(L, dtype=F32)[:, None] * inv[None, :]
    cos, sin = jnp.cos(ang), jnp.sin(ang)
    ones = jnp.ones((L, QK_NOPE), F32)
    zeros64 = jnp.zeros((L, QK_NOPE), F32)
    pad = jnp.zeros((L, HEAD_PAD - QK_DIM), F32)
    cos_t = jnp.concatenate([ones, cos, cos, pad], axis=1)
    sin_t = jnp.concatenate([zeros64, sin, sin, pad], axis=1)
    return cos_t, sin_t


def _rotary_partner(w):
    k = w.shape[0]
    half = QK_ROPE // 2
    w = w.reshape(k, HEADS, HEAD_PAD)
    first, second = w[:, :, QK_NOPE:QK_NOPE + half], w[:, :, QK_NOPE + half:QK_DIM]
    out = jnp.concatenate([jnp.zeros((k, HEADS, QK_NOPE), w.dtype), -second, first,
                           jnp.zeros((k, HEADS, HEAD_PAD - QK_DIM), w.dtype)], axis=2)
    return out.reshape(k, HEADS * HEAD_PAD)


def _pad_heads(w, width):
    k = w.shape[0]
    w = w.reshape(k, HEADS, width)
    return jnp.pad(w, ((0, 0), (0, 0), (0, HEAD_PAD - width))).reshape(k, HEADS * HEAD_PAD)


def _prepare_weights(mix_norm, ffn_norm, ffn_w_gate, ffn_w_up, ffn_w_down,
                     a_w_in, a_conv_w, a_q_a_norm, a_kv_a_norm, a_w_q_up, a_w_kv_up, a_q_norm, a_k_norm,
                     a_w_out, c_w_in, c_short_w, c_f_w1, c_f_b1, c_f_w2, c_f_b2, c_f_w3, c_f_b3, c_f_freq,
                     c_f_w4, c_bias, c_w_out):
    p = {}
    p["mix_g"] = [mix_norm[i].reshape(1, D) for i in range(2)]
    p["ffn"] = [(ffn_norm[i].reshape(1, D), ffn_w_gate[i].astype(BF16), ffn_w_up[i].astype(BF16),
                 ffn_w_down[i].astype(BF16)) for i in range(2)]
    p["a_w_in"] = jnp.pad(a_w_in[0], ((0, 0), (0, IN_A_PAD - a_w_in.shape[2]))).astype(BF16)
    p["a_conv_w"] = a_conv_w[0].T
    p["qan"] = a_q_a_norm[0].reshape(1, Q_LORA)
    p["kvan"] = a_kv_a_norm[0].reshape(1, KV_LORA)
    wq = _pad_heads(a_w_q_up[0], QK_DIM)
    p["wq"], p["wq_rot"] = wq.astype(BF16), _rotary_partner(wq).astype(BF16)
    kv = a_w_kv_up[0].reshape(KV_LORA, HEADS, QK_NOPE + V_DIM)
    p["wk"] = _pad_heads(kv[:, :, :QK_NOPE].reshape(KV_LORA, HEADS * QK_NOPE), QK_NOPE).astype(BF16)
    p["wv"] = _pad_heads(kv[:, :, QK_NOPE:].reshape(KV_LORA, HEADS * V_DIM), V_DIM).astype(BF16)
    v_ones = np.zeros((1, HEADS, HEAD_PAD), np.float32)
    v_ones[:, :, V_DIM:] = 1.0
    p["v_ones"] = jnp.asarray(v_ones.reshape(1, HEADS * HEAD_PAD))
    place = np.zeros((HEAD_PAD, HEADS, HEAD_PAD), np.float32)
    for r in range(QK_ROPE):
        place[r, :, QK_NOPE + r] = 1.0
    wpe = jnp.asarray(place.reshape(HEAD_PAD, HEADS * HEAD_PAD))
    p["wpe"], p["wpe_rot"] = wpe.astype(BF16), _rotary_partner(wpe).astype(BF16)
    half = QK_ROPE // 2
    for name, gain in (("qg", a_q_norm[0]), ("kg", a_k_norm[0])):
        p[name] = jnp.pad(gain, (0, HEAD_PAD - QK_DIM)).reshape(1, HEAD_PAD)
        swapped = jnp.concatenate([jnp.zeros((QK_NOPE,), F32), gain[QK_NOPE + half:], gain[QK_NOPE:QK_NOPE + half]])
        p[name + "_rot"] = jnp.pad(swapped, (0, HEAD_PAD - QK_DIM)).reshape(1, HEAD_PAD)
    p["ones"] = jnp.ones((HEAD_PAD, HEAD_PAD), BF16)
    p["a_w_out"] = a_w_out[0].astype(BF16)
    p["c_w_in"] = c_w_in[0].astype(BF16)
    p["c_short_w"] = c_short_w[0].T
    hid = FILTER_HIDDEN
    blockdiag = lambda w: jnp.kron(jnp.eye(2, dtype=F32), w)
    p["f_w1"] = blockdiag(jnp.pad(c_f_w1[0], ((0, hid - FILTER_EMB), (0, 0))))
    p["f_w2"], p["f_w3"] = blockdiag(c_f_w2[0]), blockdiag(c_f_w3[0])
    p["f_b"] = [jnp.tile(b[0].reshape(1, hid), (1, 2)) for b in (c_f_b1, c_f_b2, c_f_b3)]
    p["f_freq"] = jnp.tile(c_f_freq[0], (1, 2))
    w4 = jnp.tile(c_f_w4[0], (2, 1))
    p["f_w4_hi"] = w4.astype(BF16)
    p["f_w4_lo"] = (w4 - p["f_w4_hi"].astype(F32)).astype(BF16)
    p["c_bias"] = c_bias[0].reshape(1, D)
    p["c_w_out"] = c_w_out[0].astype(BF16)
    bands = np.linspace(1e-4, FILTER_BANDS - 1, FILTER_BANDS, dtype=np.float32)
    band = np.zeros((1, hid), np.float32)
    phase = np.zeros((1, hid), np.float32)
    band[0, 1:1 + FILTER_BANDS] = bands
    band[0, 1 + FILTER_BANDS:FILTER_EMB] = bands
    phase[0, 1 + FILTER_BANDS:FILTER_EMB] = np.pi / 2
    p["band"], p["phase"] = jnp.asarray(np.tile(band, (1, 2))), jnp.asarray(np.tile(phase, (1, 2)))
    max_decay = math.log(1e-2) / 0.3
    min_decay = math.log(1e-2) / 1.5
    p["decay"] = jnp.asarray(np.abs(np.linspace(min_decay, max_decay, D, dtype=np.float32)).reshape(1, D))
    return p


def _trunk(x, p):
    B, L, _ = x.shape
    x2d = x.reshape(B * L, D)
    cos_t, sin_t = _rope_tables(L)
    gb, gg, q, k, v = _l0_in(x2d, L, p, cos_t, sin_t)
    att = _flash(q, k, v, B, L)
    x2d = _l0_tail(x2d, L, gb, gg, p["a_conv_w"], att, p["a_w_out"], *p["ffn"][0])
    x0, z = _l1_in(x2d, L, p["mix_g"][1], p["c_w_in"], p["c_short_w"])
    filter_params = (p["band"], p["phase"], p["f_w1"], p["f_b"][0], p["f_w2"], p["f_b"][1], p["f_w3"],
                     p["f_b"][2], p["f_freq"], p["f_w4_hi"], p["f_w4_lo"], p["decay"])
    y = _fft_conv(z, filter_params, B, L)
    x2d = _l1_tail(x2d, y, z, x0, p["c_bias"], p["c_w_out"], *p["ffn"][1])
    return x2d.reshape(B, L, D)


def kernel(x_prompt, x_sample, mix_norm, ffn_norm, ffn_w_gate, ffn_w_up, ffn_w_down, a_w_in, a_conv_w, a_q_a_norm, a_kv_a_norm, a_w_q_up, a_w_kv_up, a_q_norm, a_k_norm, a_w_out, c_w_in, c_short_w, c_f_w1, c_f_b1, c_f_w2, c_f_b2, c_f_w3, c_f_b3, c_f_freq, c_f_w4, c_bias, c_w_out):
    p = _prepare_weights(mix_norm, ffn_norm, ffn_w_gate, ffn_w_up, ffn_w_down, a_w_in, a_conv_w,
                         a_q_a_norm, a_kv_a_norm, a_w_q_up, a_w_kv_up, a_q_norm, a_k_norm, a_w_out,
                         c_w_in, c_short_w, c_f_w1, c_f_b1, c_f_w2, c_f_b2, c_f_w3, c_f_b3, c_f_freq,
                         c_f_w4, c_bias, c_w_out)
    return (_trunk(x_prompt, p), _trunk(x_sample, p))
```

```python
import functools
import math

import jax
import jax.numpy as jnp
import numpy as np
from jax import lax
from jax.experimental import pallas as pl
from jax.experimental.pallas import tpu as pltpu

F32 = jnp.float32
BF16 = jnp.bfloat16

D = 1024
EPS = 1e-6
CONV_W = 512
HEADS = 8
QK_NOPE = 64
QK_ROPE = 32
V_DIM = 64
QK_DIM = 96
Q_LORA = 256
KV_LORA = 128
ROPE_THETA = 10000.0
LOG2E = 1.4426950408889634
HEAD_PAD = 128
IN_A_PAD = 2048
FILTER_BANDS = 16
FILTER_EMB = 33
FILTER_HIDDEN = 64
D_FF = 2816
MXU_DIM = 256
FFN_SPLITS = (0, (D_FF // MXU_DIM + 1) // 2 * MXU_DIM, D_FF)
FFT_N2 = 256

VMEM_LIMIT_BYTES = 56 * 1024 * 1024


def _params():
    return pltpu.CompilerParams(vmem_limit_bytes=VMEM_LIMIT_BYTES)


def _const_spec(shape):
    zeros = (0,) * len(shape)
    return pl.BlockSpec(shape, lambda *_: zeros, pipeline_mode=pl.Buffered(1))


def _dot(a, b):
    return jnp.dot(a, b, preferred_element_type=F32)


def _rms(x, g):
    ms = jnp.mean(x * x, axis=-1, keepdims=True)
    return x * lax.rsqrt(ms + EPS) * g


def _l0_in_kernel(x_ref, g_ref, w_ref, qan_ref, kvan_ref, wq_ref, wqr_ref, wk_ref, wpe_ref, wper_ref,
                  wv_ref, vone_ref, qg_ref, qgr_ref, kg_ref, kgr_ref, ones_ref, cos_ref, sin_ref,
                  gb_ref, gg_ref, q_ref, k_ref, v_ref):
    h = _rms(x_ref[...], g_ref[...]).astype(BF16)
    proj = _dot(h, w_ref[...])
    gb_ref[...] = proj[:, 0:CONV_W]
    gg_ref[...] = proj[:, CONV_W:2 * CONV_W] * proj[:, 2 * CONV_W:3 * CONV_W]
    c0 = 3 * CONV_W
    qn = _rms(proj[:, c0:c0 + Q_LORA], qan_ref[...]).astype(BF16)
    kvn = _rms(proj[:, c0 + Q_LORA:c0 + Q_LORA + KV_LORA], kvan_ref[...]).astype(BF16)
    pe = proj[:, c0 + Q_LORA + KV_LORA:IN_A_PAD].astype(BF16)
    qf, qr = _dot(qn, wq_ref[...]), _dot(qn, wqr_ref[...])
    kf, kr = _dot(kvn, wk_ref[...]) + _dot(pe, wpe_ref[...]), _dot(pe, wper_ref[...])
    v_ref[...] = (_dot(kvn, wv_ref[...]) + vone_ref[...]).astype(BF16)
    cos, sin = cos_ref[...], sin_ref[...]
    ones = ones_ref[...]
    twice = lambda t: jnp.concatenate([t, t], axis=1)
    sides = ((qf, qr, twice(cos * qg_ref[...]), twice(sin * qgr_ref[...]), q_ref, QK_DIM ** -0.5 * LOG2E),
             (kf, kr, twice(cos * kg_ref[...]), twice(sin * kgr_ref[...]), k_ref, 1.0))
    for pair in range(HEADS // 2):
        sl = slice(2 * HEAD_PAD * pair, 2 * HEAD_PAD * (pair + 1))
        for full, rot, cg, sg, out_ref, mult in sides:
            xh = full[:, sl]
            ss = _dot((xh * xh).astype(BF16), ones) * (1.0 / QK_DIM)
            out_ref[:, sl] = (lax.rsqrt(ss + EPS) * mult * (xh * cg + rot[:, sl] * sg)).astype(BF16)


def _l0_in(x2d, L, p, cos_t, sin_t, tm=512):
    T = x2d.shape[0]
    tps = L // tm
    row = lambda i: (i, 0)
    pos = lambda i: (i % tps, 0)
    wide = HEADS * HEAD_PAD
    return pl.pallas_call(
        _l0_in_kernel,
        grid=(T // tm,),
        in_specs=[
            pl.BlockSpec((tm, D), row),
            _const_spec((1, D)),
            _const_spec((D, IN_A_PAD)),
            _const_spec((1, Q_LORA)),
            _const_spec((1, KV_LORA)),
            _const_spec((Q_LORA, wide)),
            _const_spec((Q_LORA, wide)),
            _const_spec((KV_LORA, wide)),
            _const_spec((HEAD_PAD, wide)),
            _const_spec((HEAD_PAD, wide)),
            _const_spec((KV_LORA, wide)),
            _const_spec((1, wide)),
            _const_spec((1, HEAD_PAD)),
            _const_spec((1, HEAD_PAD)),
            _const_spec((1, HEAD_PAD)),
            _const_spec((1, HEAD_PAD)),
            _const_spec((2 * HEAD_PAD, 2 * HEAD_PAD)),
            pl.BlockSpec((tm, HEAD_PAD), pos),
            pl.BlockSpec((tm, HEAD_PAD), pos),
        ],
        out_specs=[
            pl.BlockSpec((tm, CONV_W), row),
            pl.BlockSpec((tm, CONV_W), row),
            pl.BlockSpec((tm, wide), row),
            pl.BlockSpec((tm, wide), row),
            pl.BlockSpec((tm, wide), row),
        ],
        out_shape=[
            jax.ShapeDtypeStruct((T, CONV_W), F32),
            jax.ShapeDtypeStruct((T, CONV_W), F32),
            jax.ShapeDtypeStruct((T, wide), BF16),
            jax.ShapeDtypeStruct((T, wide), BF16),
            jax.ShapeDtypeStruct((T, wide), BF16),
        ],
        compiler_params=_params(),
        name="l0_in",
    )(x2d, p["mix_g"][0], p["a_w_in"], p["qan"], p["kvan"], p["wq"], p["wq_rot"], p["wk"], p["wpe"],
      p["wpe_rot"], p["wv"], p["v_ones"], p["qg"], p["qg_rot"], p["kg"], p["kg_rot"], p["ones"],
      cos_t, sin_t)


def _flash_kernel(q_ref, k_ref, v_ref, o_ref, m_ref, acc_ref, *, nk):
    kv = pl.program_id(2)
    tq = q_ref.shape[0]

    @pl.when(kv == 0)
    def _():
        m_ref[...] = jnp.full(m_ref.shape, -jnp.inf, F32)
        acc_ref[...] = jnp.zeros(acc_ref.shape, F32)

    for hd in range(HEADS):
        sl = slice(HEAD_PAD * hd, HEAD_PAD * (hd + 1))
        s = lax.dot_general(q_ref[:, sl], k_ref[:, sl], (((1,), (1,)), ((), ())),
                            preferred_element_type=F32)
        m_prev = m_ref[hd]
        m_new = jnp.maximum(m_prev, jnp.max(s, axis=-1, keepdims=True))
        p = jnp.exp2((s - m_new[:, 0:1]).astype(BF16))
        acc_ref[:, sl] = acc_ref[:, sl] * jnp.exp2(m_prev - m_new) + _dot(p, v_ref[:, sl])
        m_ref[hd] = m_new

    @pl.when(kv == nk - 1)
    def _():
        low_half = lax.broadcasted_iota(jnp.int32, (tq, HEAD_PAD), 1) < V_DIM
        for pair in range(HEADS // 2):
            a = acc_ref[:, HEAD_PAD * 2 * pair:HEAD_PAD * (2 * pair + 1)]
            b = acc_ref[:, HEAD_PAD * (2 * pair + 1):HEAD_PAD * (2 * pair + 2)]
            o_ref[:, HEAD_PAD * pair:HEAD_PAD * (pair + 1)] = jnp.where(
                low_half, a / pltpu.roll(a, V_DIM, 1), pltpu.roll(b, V_DIM, 1) / b).astype(o_ref.dtype)


def _flash(q, k, v, B, L, tq=2048, tk=512):
    tq = min(tq, L)
    tk = min(tk, L)
    nq, nk = L // tq, L // tk
    wide = HEADS * HEAD_PAD
    return pl.pallas_call(
        functools.partial(_flash_kernel, nk=nk),
        grid=(B, nq, nk),
        in_specs=[
            pl.BlockSpec((tq, wide), lambda b, i, j: (b * nq + i, 0)),
            pl.BlockSpec((tk, wide), lambda b, i, j: (b * nk + j, 0)),
            pl.BlockSpec((tk, wide), lambda b, i, j: (b * nk + j, 0)),
        ],
        out_specs=pl.BlockSpec((tq, HEADS * V_DIM), lambda b, i, j: (b * nq + i, 0)),
        out_shape=jax.ShapeDtypeStruct((B * L, HEADS * V_DIM), BF16),
        scratch_shapes=[
            pltpu.VMEM((HEADS, tq, HEAD_PAD), F32),
            pltpu.VMEM((tq, wide), F32),
        ],
        compiler_params=_params(),
        name="flash_attention",
    )(q, k, v)


def _ffn(x1, fg_ref, wg_ref, wu_ref, wd_ref):
    hn = _rms(x1, fg_ref[...]).astype(BF16)
    out = x1
    for lo, hi in zip(FFN_SPLITS[:-1], FFN_SPLITS[1:]):
        cols = slice(lo, hi)
        g = _dot(hn, wg_ref[:, cols])
        u = _dot(hn, wu_ref[:, cols])
        a = (g * (1.0 / (1.0 + jnp.exp(-g))) * u).astype(BF16)
        out = out + _dot(a, wd_ref[cols, :])
    return out


def _shift_rows(cur, prev_row, next_row):
    tm = cur.shape[0]
    row = lax.broadcasted_iota(jnp.int32, cur.shape, 0)
    before = jnp.where(row == 0, prev_row, pltpu.roll(cur, 1, 0))
    after = jnp.where(row == tm - 1, next_row, pltpu.roll(cur, tm - 1, 0))
    return before, after


def _halo_rows(prev_ref, next_ref, cols, tiles_per_seq):
    i = pl.program_id(0) % tiles_per_seq
    prev_row = jnp.where(i == 0, 0.0, prev_ref[7:8, cols])
    next_row = jnp.where(i == tiles_per_seq - 1, 0.0, next_ref[0:1, cols])
    return prev_row, next_row


def _l0_tail_kernel(x_ref, gb_ref, gg_ref, gprev_ref, gnext_ref, cw_ref, att_ref, wo_ref,
                    fg_ref, wg_ref, wu_ref, wd_ref, o_ref, *, tiles_per_seq):
    gg = gg_ref[...]
    prev_row, next_row = _halo_rows(gprev_ref, gnext_ref, slice(None), tiles_per_seq)
    before, after = _shift_rows(gg, prev_row, next_row)
    cw = cw_ref[...]
    y_conv = gb_ref[...] * (before * cw[0:1] + gg * cw[1:2] + after * cw[2:3])
    mix = _dot(y_conv.astype(BF16), wo_ref[0:CONV_W, :]) + _dot(att_ref[...], wo_ref[CONV_W:, :])
    o_ref[...] = _ffn(x_ref[...] + mix, fg_ref, wg_ref, wu_ref, wd_ref)


def _halo_specs(tm, n_rows, width):
    blocks = tm // 8
    last = n_rows // 8 - 1
    return (pl.BlockSpec((8, width), lambda i: (jnp.maximum(i * blocks - 1, 0), 0)),
            pl.BlockSpec((8, width), lambda i: (jnp.minimum((i + 1) * blocks, last), 0)))


def _ffn_specs(layer):
    pick = lambda *_: (layer, 0, 0)
    return [pl.BlockSpec((None,) + shape, pick, pipeline_mode=pl.Buffered(1))
            for shape in ((1, D), (D, D_FF), (D, D_FF), (D_FF, D))]


def _l0_tail(x2d, L, gb, gg, conv_w, att, w_out, fg, wg, wu, wd, tm=512):
    T = x2d.shape[0]
    row = lambda i: (i, 0)
    prev_spec, next_spec = _halo_specs(tm, T, CONV_W)
    return pl.pallas_call(
        functools.partial(_l0_tail_kernel, tiles_per_seq=L // tm),
        grid=(T // tm,),
        in_specs=[
            pl.BlockSpec((tm, D), row),
            pl.BlockSpec((tm, CONV_W), row),
            pl.BlockSpec((tm, CONV_W), row),
            prev_spec,
            next_spec,
            _const_spec((3, CONV_W)),
            pl.BlockSpec((tm, HEADS * V_DIM), row),
            _const_spec((D, D)),
        ] + _ffn_specs(0),
        out_specs=pl.BlockSpec((tm, D), row),
        out_shape=jax.ShapeDtypeStruct((T, D), F32),
        compiler_params=_params(),
        name="l0_tail",
    )(x2d, gb, gg, gg, gg, conv_w, att, w_out, fg, wg, wu, wd)


def _l1_tail_kernel(x_ref, y_ref, z_ref, x0_ref, bias_ref, wo_ref, fg_ref, wg_ref, wu_ref, wd_ref, o_ref):
    gated = (y_ref[...] + z_ref[...] * bias_ref[...]) * x0_ref[...]
    x1 = x_ref[...] + _dot(gated.astype(BF16), wo_ref[...])
    o_ref[...] = _ffn(x1, fg_ref, wg_ref, wu_ref, wd_ref)


def _l1_tail(x2d, y, z, x0, bias, w_out, fg, wg, wu, wd, tm=512):
    T = x2d.shape[0]
    row = lambda i: (i, 0)
    tile = pl.BlockSpec((tm, D), row)
    return pl.pallas_call(
        _l1_tail_kernel,
        grid=(T // tm,),
        in_specs=[tile, tile, tile, tile, _const_spec((1, D)), _const_spec((D, D))] + _ffn_specs(1),
        out_specs=tile,
        out_shape=jax.ShapeDtypeStruct((T, D), F32),
        compiler_params=_params(),
        name="l1_tail",
    )(x2d, y, z, x0, bias, w_out, fg, wg, wu, wd)


def _l1_in_kernel(x_ref, xprev_ref, xnext_ref, g_ref, w_ref, sw_ref, x0_ref, z_ref, *, tiles_per_seq):
    tm = x_ref.shape[0]
    i = pl.program_id(0) % tiles_per_seq
    keep_prev = jnp.where(i == 0, 0.0, 1.0)
    keep_next = jnp.where(i == tiles_per_seq - 1, 0.0, 1.0)
    x_ext = jnp.concatenate([xprev_ref[...] * keep_prev, x_ref[...], xnext_ref[...] * keep_next], axis=0)
    h = _rms(x_ext, g_ref[...]).astype(BF16)
    rows = slice(8, 8 + tm)

    def conv(c):
        cols = slice(D * c, D * (c + 1))
        p = _dot(h, w_ref[:, cols])
        before = pltpu.roll(p, 1, 0)[rows]
        after = pltpu.roll(p, tm + 15, 0)[rows]
        return before * sw_ref[0:1, cols] + p[rows] * sw_ref[1:2, cols] + after * sw_ref[2:3, cols]

    x0_ref[...] = conv(0)
    z_ref[...] = conv(2) * conv(1)


def _l1_in(x2d, L, g, w, short_w, tm=512):
    T = x2d.shape[0]
    row = lambda i: (i, 0)
    prev_spec, next_spec = _halo_specs(tm, T, D)
    return pl.pallas_call(
        functools.partial(_l1_in_kernel, tiles_per_seq=L // tm),
        grid=(T // tm,),
        in_specs=[pl.BlockSpec((tm, D), row), prev_spec, next_spec, _const_spec((1, D)),
                  _const_spec((D, 3 * D)), _const_spec((3, 3 * D))],
        out_specs=[pl.BlockSpec((tm, D), row), pl.BlockSpec((tm, D), row)],
        out_shape=[jax.ShapeDtypeStruct((T, D), F32), jax.ShapeDtypeStruct((T, D), F32)],
        compiler_params=_params(),
        name="l1_in",
    )(x2d, x2d, x2d, g, w, short_w)


def _filter_kernel(band_ref, phase_ref, w1_ref, b1_ref, w2_ref, b2_ref, w3_ref, b3_ref, fr_ref,
                   w4h_ref, w4l_ref, decay_ref, a_ref, o_ref, *, L):
    half = a_ref.shape[1] // 2
    row = lax.broadcasted_iota(jnp.int32, (half, 1), 0)
    groups = []
    for g in range(2):
        rho = row + g * half
        m = (rho >> 3) * FFT_N2 + (rho & 7) + 8 * pl.program_id(0)
        j = m.astype(F32)
        groups.append((m, j / (L - 1.0), (2.0 * math.pi) * j / L))
    lane = lax.broadcasted_iota(jnp.int32, (1, HEAD_PAD), 1)
    first = lane < FILTER_HIDDEN
    t = jnp.where(first, groups[0][1], groups[1][1])
    w = jnp.where(first, groups[0][2], groups[1][2])
    feats = jnp.where((lane & (FILTER_HIDDEN - 1)) == 0, t, jnp.cos(w * band_ref[...] + phase_ref[...]))
    hp = lax.Precision.HIGHEST
    z = jnp.sin(fr_ref[0:1, :] * (jnp.dot(feats, w1_ref[...], precision=hp, preferred_element_type=F32)
                                  + b1_ref[...]))
    z = jnp.sin(fr_ref[1:2, :] * (jnp.dot(z, w2_ref[...], precision=hp, preferred_element_type=F32)
                                  + b2_ref[...]))
    z = jnp.sin(fr_ref[2:3, :] * (jnp.dot(z, w3_ref[...], precision=hp, preferred_element_type=F32)
                                  + b3_ref[...]))
    taps = ([], [])
    for g, (m, tg, _) in enumerate(groups):
        zg = jnp.where(first, z, 0.0) if g == 0 else jnp.where(first, 0.0, z)
        z_hi = zg.astype(BF16)
        z_lo = (zg - z_hi.astype(F32)).astype(BF16)
        h = _dot(z_hi, w4h_ref[...]) + _dot(z_hi, w4l_ref[...]) + _dot(z_lo, w4h_ref[...])
        window = jnp.exp(-tg * decay_ref[...])
        taps[0].append((h[:, :D] * window).astype(BF16))
        taps[1].append((h[:, D:] * window * jnp.where(m == 0, 0.0, 1.0)).astype(BF16))
    for i in range(2):
        x1 = _dot(a_ref[:, :half], taps[i][0]) + _dot(a_ref[:, half:], taps[i][1])
        o_ref[i] = x1.reshape(o_ref.shape[1:])


def _filter_stage1(L, a8, band, phase, w1, b1, w2, b2, w3, b3, freq, w4_hi, w4_lo, decay):
    hid = 2 * FILTER_HIDDEN
    rows_out, rows_in = a8.shape
    return pl.pallas_call(
        functools.partial(_filter_kernel, L=L),
        grid=(FFT_N2 // 8,),
        in_specs=[
            _const_spec((1, HEAD_PAD)), _const_spec((1, HEAD_PAD)),
            _const_spec((HEAD_PAD, hid)), _const_spec((1, hid)),
            _const_spec((hid, hid)), _const_spec((1, hid)),
            _const_spec((hid, hid)), _const_spec((1, hid)),
            _const_spec((3, hid)),
            _const_spec((hid, 2 * D)),
            _const_spec((hid, 2 * D)),
            _const_spec((1, D)),
            _const_spec((rows_out, rows_in)),
        ],
        out_specs=pl.BlockSpec((2, rows_out // 8, 8, D), lambda i: (0, 0, i, 0)),
        out_shape=jax.ShapeDtypeStruct((2, rows_out // 8, FFT_N2, D), F32),
        compiler_params=_params(),
        name="hyena_filter",
    )(band, phase, w1, b1, w2, b2, w3, b3, freq, w4_hi, w4_lo, decay, a8)


def _stage1_kernel(a_ref, x_ref, o_ref):
    k, r, c = x_ref.shape
    a = a_ref[...]
    for i in range(r // 8):
        rows = slice(8 * i, 8 * (i + 1))
        x = x_ref[:, rows, :].reshape(k * 8, c).astype(BF16)
        o_ref[:, rows, :] = _dot(a, x).reshape(o_ref.shape[0], 8, c)


def _stage1(a8, x, cols=1024, rows=16):
    B, K, R, _ = x.shape
    M = a8.shape[0] // 8
    return pl.pallas_call(
        _stage1_kernel,
        grid=(B, R // rows, D // cols),
        in_specs=[_const_spec((8 * M, 8 * K)),
                  pl.BlockSpec((None, K, rows, cols), lambda b, j, c: (b, 0, j, c))],
        out_specs=pl.BlockSpec((None, M, rows, cols), lambda b, j, c: (b, 0, j, c)),
        out_shape=jax.ShapeDtypeStruct((B, M, R, D), F32),
        compiler_params=_params(),
        name="fft_stage1",
    )(a8, x)


def _complex_block(m):
    top, bot = m[:FFT_N2], m[FFT_N2:]
    return jnp.concatenate([m, jnp.concatenate([-bot, top], axis=0)], axis=1)


def _fft_mid_kernel(mf_ref, f_ref, x_ref, o_ref, kf_ref):
    g = _complex_block(mf_ref[...].astype(BF16))

    @pl.when(pl.program_id(1) == 0)
    def _():
        zf = _dot(g, f_ref[0].reshape(2 * FFT_N2, D).astype(BF16))
        zb = _dot(g, f_ref[1].reshape(2 * FFT_N2, D).astype(BF16))
        kf_ref[0] = zf[:FFT_N2] + zb[:FFT_N2]
        kf_ref[1] = zf[FFT_N2:] - zb[FFT_N2:]

    x = x_ref[...].reshape(2 * FFT_N2, D).astype(BF16)
    zf = _dot(g, x)
    zr, zi = zf[:FFT_N2], zf[FFT_N2:]
    kr, ki = kf_ref[0], kf_ref[1]
    y = jnp.concatenate([zr * kr - zi * ki, zr * ki + zi * kr], axis=0).astype(BF16)
    yi = lax.dot_general(g, y, (((0,), (0,)), ((), ())), preferred_element_type=F32)
    o_ref[...] = yi.reshape(2, FFT_N2, D)


def _fft_mid(mf, f1, x1):
    B, _, H = x1.shape[:3]
    return pl.pallas_call(
        _fft_mid_kernel,
        grid=(H, B),
        in_specs=[pl.BlockSpec((None, 2 * FFT_N2, FFT_N2), lambda k, b: (k, 0, 0)),
                  pl.BlockSpec((2, 2, None, FFT_N2, D), lambda k, b: (0, 0, k, 0, 0)),
                  pl.BlockSpec((None, 2, None, FFT_N2, D), lambda k, b: (b, 0, k, 0, 0))],
        out_specs=pl.BlockSpec((None, 2, None, FFT_N2, D), lambda k, b: (b, 0, k, 0, 0)),
        out_shape=jax.ShapeDtypeStruct(x1.shape, F32),
        scratch_shapes=[pltpu.VMEM((2, FFT_N2, D), F32)],
        compiler_params=_params(),
        name="fft_mid",
    )(mf, f1, x1)


@functools.lru_cache(maxsize=None)
def _fft_constants(L):
    N = 2 * L
    N1 = N // FFT_N2
    H = N1 // 2
    k1 = np.arange(H, dtype=np.float64) + 0.5
    th = 2.0 * np.pi * np.outer(k1, np.arange(N1, dtype=np.float64)) / N1
    a_full = np.concatenate([np.cos(th), -np.sin(th)], axis=0)
    a_half = a_full[:, :H]
    thi = th[:, :H].T
    b_inv = (2.0 / N) * np.concatenate([np.cos(thi), -np.sin(thi)], axis=1)
    n2 = np.arange(FFT_N2, dtype=np.float64)
    freq = k1[:, None, None] + N1 * n2[None, :, None]
    ang = 2.0 * np.pi * (freq * n2[None, None, :] % N) / N
    mf = np.concatenate([np.cos(ang), -np.sin(ang)], axis=1)
    eye8 = np.eye(8)
    mats = (np.kron(a_half, eye8), np.kron(b_inv, eye8), mf)
    return tuple(v.astype(np.float32) for v in mats)


def _fft_conv(z, filter_params, B, L):
    H = L // FFT_N2
    a_half8, b_inv8, mf = _fft_constants(L)
    a_half8, b_inv8 = jnp.asarray(a_half8).astype(BF16), jnp.asarray(b_inv8).astype(BF16)
    f1 = _filter_stage1(L, a_half8, *filter_params).reshape(2, 2, H, FFT_N2, D)
    x1 = _stage1(a_half8, z.reshape(B, H, FFT_N2, D)).reshape(B, 2, H, FFT_N2, D)
    y1 = _fft_mid(jnp.asarray(mf), f1, x1).reshape(B, 2 * H, FFT_N2, D)
    return _stage1(b_inv8, y1).reshape(B * L, D)


@functools.lru_cache(maxsize=None)
def _rope_tables(L):
    half = QK_ROPE // 2
    inv = ROPE_THETA ** (-np.arange(half, dtype=np.float64) / half)
    ang = np.arange(L, dtype=np.float64)[:, None] * inv[None, :]
    cos, sin = np.cos(ang), np.sin(ang)
    ones = np.ones((L, QK_NOPE))
    zeros64 = np.zeros((L, QK_NOPE))
    pad = np.zeros((L, HEAD_PAD - QK_DIM))
    cos_t = np.concatenate([ones, cos, cos, pad], axis=1).astype(np.float32)
    sin_t = np.concatenate([zeros64, sin, sin, pad], axis=1).astype(np.float32)
    return cos_t, sin_t


def _rotary_partner(w):
    k = w.shape[0]
    half = QK_ROPE // 2
    w = w.reshape(k, HEADS, HEAD_PAD)
    first, second = w[:, :, QK_NOPE:QK_NOPE + half], w[:, :, QK_NOPE + half:QK_DIM]
    out = jnp.concatenate([jnp.zeros((k, HEADS, QK_NOPE), w.dtype), -second, first,
                           jnp.zeros((k, HEADS, HEAD_PAD - QK_DIM), w.dtype)], axis=2)
    return out.reshape(k, HEADS * HEAD_PAD)


def _pad_heads(w, width):
    k = w.shape[0]
    w = w.reshape(k, HEADS, width)
    return jnp.pad(w, ((0, 0), (0, 0), (0, HEAD_PAD - width))).reshape(k, HEADS * HEAD_PAD)


def _prepare_weights(mix_norm, ffn_norm, ffn_w_gate, ffn_w_up, ffn_w_down,
                     a_w_in, a_conv_w, a_q_a_norm, a_kv_a_norm, a_w_q_up, a_w_kv_up, a_q_norm, a_k_norm,
                     a_w_out, c_w_in, c_short_w, c_f_w1, c_f_b1, c_f_w2, c_f_b2, c_f_w3, c_f_b3, c_f_freq,
                     c_f_w4, c_bias, c_w_out):
    p = {}
    p["mix_g"] = [mix_norm[i].reshape(1, D) for i in range(2)]
    p["ffn"] = (ffn_norm.reshape(2, 1, D), ffn_w_gate.astype(BF16), ffn_w_up.astype(BF16), ffn_w_down.astype(BF16))
    p["a_w_in"] = jnp.pad(a_w_in[0], ((0, 0), (0, IN_A_PAD - a_w_in.shape[2]))).astype(BF16)
    p["a_conv_w"] = a_conv_w[0].T
    p["qan"] = a_q_a_norm[0].reshape(1, Q_LORA)
    p["kvan"] = a_kv_a_norm[0].reshape(1, KV_LORA)
    wq = _pad_heads(a_w_q_up[0], QK_DIM)
    p["wq"], p["wq_rot"] = wq.astype(BF16), _rotary_partner(wq).astype(BF16)
    kv = a_w_kv_up[0].reshape(KV_LORA, HEADS, QK_NOPE + V_DIM)
    p["wk"] = _pad_heads(kv[:, :, :QK_NOPE].reshape(KV_LORA, HEADS * QK_NOPE), QK_NOPE).astype(BF16)
    p["wv"] = _pad_heads(kv[:, :, QK_NOPE:].reshape(KV_LORA, HEADS * V_DIM), V_DIM).astype(BF16)
    v_ones = np.zeros((1, HEADS, HEAD_PAD), np.float32)
    v_ones[:, :, V_DIM:] = 1.0
    p["v_ones"] = jnp.asarray(v_ones.reshape(1, HEADS * HEAD_PAD))
    place = np.zeros((HEAD_PAD, HEADS, HEAD_PAD), np.float32)
    for r in range(QK_ROPE):
        place[r, :, QK_NOPE + r] = 1.0
    wpe = jnp.asarray(place.reshape(HEAD_PAD, HEADS * HEAD_PAD))
    p["wpe"], p["wpe_rot"] = wpe.astype(BF16), _rotary_partner(wpe).astype(BF16)
    half = QK_ROPE // 2
    for name, gain in (("qg", a_q_norm[0]), ("kg", a_k_norm[0])):
        p[name] = jnp.pad(gain, (0, HEAD_PAD - QK_DIM)).reshape(1, HEAD_PAD)
        swapped = jnp.concatenate([jnp.zeros((QK_NOPE,), F32), gain[QK_NOPE + half:], gain[QK_NOPE:QK_NOPE + half]])
        p[name + "_rot"] = jnp.pad(swapped, (0, HEAD_PAD - QK_DIM)).reshape(1, HEAD_PAD)
    p["ones"] = jnp.asarray(np.kron(np.eye(2, dtype=np.float32), np.ones((HEAD_PAD, HEAD_PAD), np.float32))).astype(BF16)
    p["a_w_out"] = a_w_out[0].astype(BF16)
    p["c_w_in"] = c_w_in[0].astype(BF16)
    p["c_short_w"] = c_short_w[0].T
    hid = FILTER_HIDDEN
    blockdiag = lambda w: jnp.kron(jnp.eye(2, dtype=F32), w)
    p["f_w1"] = blockdiag(jnp.pad(c_f_w1[0], ((0, hid - FILTER_EMB), (0, 0))))
    p["f_w2"], p["f_w3"] = blockdiag(c_f_w2[0]), blockdiag(c_f_w3[0])
    p["f_b"] = [jnp.tile(b[0].reshape(1, hid), (1, 2)) for b in (c_f_b1, c_f_b2, c_f_b3)]
    p["f_freq"] = jnp.tile(c_f_freq[0], (1, 2))
    w4 = jnp.tile(c_f_w4[0], (2, 1))
    p["f_w4_hi"] = w4.astype(BF16)
    p["f_w4_lo"] = (w4 - p["f_w4_hi"].astype(F32)).astype(BF16)
    p["c_bias"] = c_bias[0].reshape(1, D)
    p["c_w_out"] = c_w_out[0].astype(BF16)
    bands = np.linspace(1e-4, FILTER_BANDS - 1, FILTER_BANDS, dtype=np.float32)
    band = np.zeros((1, hid), np.float32)
    phase = np.zeros((1, hid), np.float32)
    band[0, 1:1 + FILTER_BANDS] = bands
    band[0, 1 + FILTER_BANDS:FILTER_EMB] = bands
    phase[0, 1 + FILTER_BANDS:FILTER_EMB] = np.pi / 2
    p["band"], p["phase"] = jnp.asarray(np.tile(band, (1, 2))), jnp.asarray(np.tile(phase, (1, 2)))
    max_decay = math.log(1e-2) / 0.3
    min_decay = math.log(1e-2) / 1.5
    p["decay"] = jnp.asarray(np.abs(np.linspace(min_decay, max_decay, D, dtype=np.float32)).reshape(1, D))
    return p


def _trunk(x, p, rope):
    B, L, _ = x.shape
    x2d = x.reshape(B * L, D)
    gb, gg, q, k, v = _l0_in(x2d, L, p, *rope)
    att = _flash(q, k, v, B, L)
    x2d = _l0_tail(x2d, L, gb, gg, p["a_conv_w"], att, p["a_w_out"], *p["ffn"])
    x0, z = _l1_in(x2d, L, p["mix_g"][1], p["c_w_in"], p["c_short_w"])
    filter_params = (p["band"], p["phase"], p["f_w1"], p["f_b"][0], p["f_w2"], p["f_b"][1], p["f_w3"],
                     p["f_b"][2], p["f_freq"], p["f_w4_hi"], p["f_w4_lo"], p["decay"])
    y = _fft_conv(z, filter_params, B, L)
    x2d = _l1_tail(x2d, y, z, x0, p["c_bias"], p["c_w_out"], *p["ffn"])
    return x2d.reshape(B, L, D)


def kernel(x_prompt, x_sample, mix_norm, ffn_norm, ffn_w_gate, ffn_w_up, ffn_w_down, a_w_in, a_conv_w, a_q_a_norm, a_kv_a_norm, a_w_q_up, a_w_kv_up, a_q_norm, a_k_norm, a_w_out, c_w_in, c_short_w, c_f_w1, c_f_b1, c_f_w2, c_f_b2, c_f_w3, c_f_b3, c_f_freq, c_f_w4, c_bias, c_w_out):
    p = _prepare_weights(mix_norm, ffn_norm, ffn_w_gate, ffn_w_up, ffn_w_down, a_w_in, a_conv_w,
                         a_q_a_norm, a_kv_a_norm, a_w_q_up, a_w_kv_up, a_q_norm, a_k_norm, a_w_out,
                         c_w_in, c_short_w, c_f_w1, c_f_b1, c_f_w2, c_f_b2, c_f_w3, c_f_b3, c_f_freq,
                         c_f_w4, c_bias, c_w_out)
    rope = _rope_tables(max(x_prompt.shape[1], x_sample.shape[1]))
    return (_trunk(x_prompt, p, rope), _trunk(x_sample, p, rope))
```
